```python
import jax, jax.numpy as jnp
from jax import lax
import numpy as np

D_MODEL = 1024
BATCH = 8
SEQ = 8192
DEPTH = 1
DEC_BATCH = 32
DEC_SEQ = 64
PAST_LEN = 4096

CHUNK = 64
N_HEADS_SB = 8
D_HEAD_SB = 64
D_SB = N_HEADS_SB * D_HEAD_SB
POOL_WINDOWS = (2, 4, 8, 16)
N_POOL_GROUPS = 4
D_POOL_GROUP = 128
D_POOL = N_POOL_GROUPS * D_POOL_GROUP
POOL_BUF = 15
Q_BLOCK = 128
D_IN = 3 * D_SB + D_POOL + 2 * D_MODEL
N_KEYS = 128
N_EXPERTS = N_KEYS * N_KEYS
PEER_HEADS = 8
D_QUERY = 256
D_SUBKEY = D_QUERY // 2
TOPK_HALF = 16
TOPK = 16
PEER_BLOCK = 256
N_MOD = 6
EPS = 1e-6

kernel_name = "stickbreak_pool_peer_streaming_encoder"


def rmsnorm(x, g):
    x32 = x.astype(jnp.float32)
    r = x32 * lax.rsqrt(jnp.mean(x32 * x32, axis=-1, keepdims=True) + EPS)
    return (r * g.astype(jnp.float32)).astype(x.dtype)


def stick_breaking(q, k, v, q_offset):
    B, H, T, dh = q.shape
    scale = dh ** -0.5
    outs = []
    for start in range(0, T, Q_BLOCK):
        end = min(start + Q_BLOCK, T)
        n_k = q_offset + end
        qb = q[:, :, start:end].astype(jnp.float32)
        kb = k[:, :, :n_k].astype(jnp.float32)
        vb = v[:, :, :n_k].astype(jnp.float32)
        z = jnp.einsum('bhqd,bhkd->bhqk', qb, kb) * scale
        q_pos = q_offset + jnp.arange(start, end)
        k_pos = jnp.arange(n_k)
        mask = k_pos[None, :] < q_pos[:, None]
        log_keep = jnp.where(mask, jax.nn.log_sigmoid(-z), 0.0)
        suffix = lax.cumsum(log_keep, axis=3, reverse=True) - log_keep
        a = jnp.where(mask, jnp.exp(jax.nn.log_sigmoid(z) + suffix), 0.0)
        outs.append(jnp.einsum('bhqk,bhkd->bhqd', a, vb).astype(q.dtype))
    return jnp.concatenate(outs, axis=2)


def pool_mix(p, prefix, n_prefix_valid):
    B, T, _ = p.shape
    full = jnp.concatenate([prefix, p], axis=1).astype(jnp.float32)
    cs = jnp.cumsum(full, axis=1)
    cs = jnp.concatenate([jnp.zeros((B, 1, D_POOL), jnp.float32), cs], axis=1)
    t = jnp.arange(T)
    outs = []
    for g, w in enumerate(POOL_WINDOWS):
        sl = slice(g * D_POOL_GROUP, (g + 1) * D_POOL_GROUP)
        hi = cs[:, POOL_BUF + 1:POOL_BUF + 1 + T, sl]
        lo = cs[:, POOL_BUF + 1 - w:POOL_BUF + 1 - w + T, sl]
        cnt = jnp.minimum(n_prefix_valid + t + 1, w).astype(jnp.float32)
        outs.append((hi - lo) / cnt[None, :, None])
    mean = jnp.concatenate(outs, axis=-1)
    return (mean - p.astype(jnp.float32)).astype(p.dtype)


def mixer(h, past_k, past_v, pool_prefix, n_prefix_valid,
          w_in, w_pool_grp, pool_scale, w_sb_up, w_pool_up, w_out):
    B, T, _ = h.shape
    proj = h @ w_in
    q, k, v, p, g_sb, g_pool = jnp.split(
        proj, [D_SB, 2 * D_SB, 3 * D_SB, 3 * D_SB + D_POOL, 3 * D_SB + D_POOL + D_MODEL], axis=-1)
    to_heads = lambda a: a.reshape(B, T, N_HEADS_SB, D_HEAD_SB).transpose(0, 2, 1, 3)
    q, k, v = to_heads(q), to_heads(k), to_heads(v)
    if past_k is None:
        k_all, v_all, q_off = k, v, 0
    else:
        k_all = jnp.concatenate([past_k, k], axis=2)
        v_all = jnp.concatenate([past_v, v], axis=2)
        q_off = past_k.shape[2]
    o = stick_breaking(q, k_all, v_all, q_off)
    y_sb = o.transpose(0, 2, 1, 3).reshape(B, T, D_SB) @ w_sb_up
    pooled = pool_mix(p, pool_prefix, n_prefix_valid).reshape(B, T, N_POOL_GROUPS, D_POOL_GROUP)
    pooled = jnp.einsum('btgc,gcd->btgd', pooled, w_pool_grp).reshape(B, T, D_POOL) * pool_scale
    y_pool = pooled @ w_pool_up
    merged = jax.nn.sigmoid(g_sb) * y_sb + jax.nn.sigmoid(g_pool) * y_pool
    new_pool = jnp.concatenate([pool_prefix, p], axis=1)[:, -POOL_BUF:]
    return merged @ w_out, k, v, new_pool


def peer_block(xb, w_query, sub_keys, expert_u, expert_v):
    n = xb.shape[0]
    qry = (xb @ w_query).reshape(n, PEER_HEADS, 2, D_SUBKEY).astype(jnp.float32)
    s = jnp.einsum('nhpc,pkc->nhpk', qry, sub_keys.astype(jnp.float32))
    s1, i1 = lax.top_k(s[:, :, 0], TOPK_HALF)
    s2, i2 = lax.top_k(s[:, :, 1], TOPK_HALF)
    cand_s = (s1[..., :, None] + s2[..., None, :]).reshape(n, PEER_HEADS, TOPK_HALF * TOPK_HALF)
    cand_i = (i1[..., :, None] * N_KEYS + i2[..., None, :]).reshape(n, PEER_HEADS, TOPK_HALF * TOPK_HALF)
    top_s, pos = lax.top_k(cand_s, TOPK)
    idx = jnp.take_along_axis(cand_i, pos, axis=-1)
    gate = jax.nn.softmax(top_s, axis=-1)
    u = expert_u[idx]
    act = jax.nn.gelu(jnp.einsum('nhed,nd->nhe', u, xb), approximate=False)
    wgt = (gate * act.astype(jnp.float32)).astype(xb.dtype)
    return jnp.einsum('nhe,nhed->nd', wgt, expert_v[idx])


def peer(h, w_query, sub_keys, expert_u, expert_v):
    B, T, D = h.shape
    n = B * T
    flat = h.reshape(n, D)
    flat = jnp.pad(flat, ((0, (-n) % PEER_BLOCK), (0, 0)))
    blocks = flat.reshape(-1, PEER_BLOCK, D)
    out = lax.map(lambda xb: peer_block(xb, w_query, sub_keys, expert_u, expert_v), blocks)
    return out.reshape(-1, D)[:n].reshape(B, T, D)


def trunk_layer(x, c, past_k, past_v, pool_prefix, n_prefix_valid,
                w_ada, b_ada, g_mix, g_ffn, w_in, w_pool_grp, pool_scale, w_sb_up, w_pool_up, w_out,
                w_query, sub_keys, expert_u, expert_v):
    B = x.shape[0]
    mod = (c @ w_ada + b_ada).reshape(B, N_MOD, 1, D_MODEL)
    shift1, scale1, gate1, shift2, scale2, gate2 = [mod[:, i] for i in range(N_MOD)]
    h = rmsnorm(x, g_mix) * (1 + scale1) + shift1
    mix_out, k_new, v_new, pool_new = mixer(h, past_k, past_v, pool_prefix, n_prefix_valid,
                                            w_in, w_pool_grp, pool_scale, w_sb_up, w_pool_up, w_out)
    x = x + gate1 * mix_out
    h = rmsnorm(x, g_ffn) * (1 + scale2) + shift2
    x = x + gate2 * peer(h, w_query, sub_keys, expert_u, expert_v)
    return x, k_new, v_new, pool_new


def setup_inputs(seed: int = 0) -> dict:
    key = jax.random.key(seed)
    ks = jax.random.split(key, 24)
    nrm = lambda k, shape, s: jax.random.normal(k, shape, jnp.float32) * s
    return {
        "x_prompt": nrm(ks[0], (BATCH, SEQ, D_MODEL), 1.0),
        "x_sample": nrm(ks[1], (DEC_BATCH, DEC_SEQ, D_MODEL), 1.0),
        "cache_sb_k": nrm(ks[2], (DEPTH, DEC_BATCH, N_HEADS_SB, PAST_LEN, D_HEAD_SB), 1.0),
        "cache_sb_v": nrm(ks[3], (DEPTH, DEC_BATCH, N_HEADS_SB, PAST_LEN, D_HEAD_SB), 1.0),
        "state_pool": nrm(ks[4], (DEPTH, DEC_BATCH, POOL_BUF, D_POOL), 1.0),
        "c_prompt": nrm(ks[5], (BATCH, D_MODEL), 1.0),
        "c_sample": nrm(ks[6], (DEC_BATCH, D_MODEL), 1.0),
        "w_ada": nrm(ks[7], (DEPTH, D_MODEL, N_MOD * D_MODEL), 0.5 * D_MODEL ** -0.5),
        "b_ada": nrm(ks[8], (DEPTH, N_MOD * D_MODEL), 0.02),
        "norm_mix_g": 1.0 + nrm(ks[9], (DEPTH, D_MODEL), 0.02),
        "norm_ffn_g": 1.0 + nrm(ks[10], (DEPTH, D_MODEL), 0.02),
        "w_in": nrm(ks[11], (DEPTH, D_MODEL, D_IN), D_MODEL ** -0.5),
        "w_pool_grp": nrm(ks[12], (DEPTH, N_POOL_GROUPS, D_POOL_GROUP, D_POOL_GROUP), D_POOL_GROUP ** -0.5),
        "pool_scale": 1.0 + nrm(ks[13], (DEPTH, D_POOL), 0.02),
        "w_sb_up": nrm(ks[14], (DEPTH, D_SB, D_MODEL), D_SB ** -0.5),
        "w_pool_up": nrm(ks[15], (DEPTH, D_POOL, D_MODEL), D_POOL ** -0.5),
        "w_out": nrm(ks[16], (DEPTH, D_MODEL, D_MODEL), D_MODEL ** -0.5),
        "peer_w_query": nrm(ks[17], (DEPTH, D_MODEL, PEER_HEADS * D_QUERY), D_MODEL ** -0.5),
        "peer_sub_keys": nrm(ks[18], (DEPTH, 2, N_KEYS, D_SUBKEY), D_SUBKEY ** -0.5),
        "peer_u": nrm(ks[19], (DEPTH, N_EXPERTS, D_MODEL), D_MODEL ** -0.5),
        "peer_v": nrm(ks[20], (DEPTH, N_EXPERTS, D_MODEL), PEER_HEADS ** -0.5),
        "final_norm_g": 1.0 + nrm(ks[21], (D_MODEL,), 0.02),
    }


def reference(x_prompt, x_sample, cache_sb_k, cache_sb_v, state_pool, c_prompt, c_sample,
              w_ada, b_ada, norm_mix_g, norm_ffn_g, w_in, w_pool_grp, pool_scale, w_sb_up, w_pool_up,
              w_out, peer_w_query, peer_sub_keys, peer_u, peer_v, final_norm_g):
    xp, xs = x_prompt, x_sample
    kp_l, vp_l, pp_l, ks_l, vs_l, ps_l = [], [], [], [], [], []
    for l in range(DEPTH):
        weights = (w_ada[l], b_ada[l], norm_mix_g[l], norm_ffn_g[l], w_in[l], w_pool_grp[l],
                   pool_scale[l], w_sb_up[l], w_pool_up[l], w_out[l],
                   peer_w_query[l], peer_sub_keys[l], peer_u[l], peer_v[l])
        zero_prefix = jnp.zeros((xp.shape[0], POOL_BUF, D_POOL), xp.dtype)
        xp, kp, vp, pp = trunk_layer(xp, c_prompt, None, None, zero_prefix, 0, *weights)
        xs, kn, vn, pn = trunk_layer(xs, c_sample, cache_sb_k[l], cache_sb_v[l], state_pool[l], POOL_BUF, *weights)
        kp_l.append(kp); vp_l.append(vp); pp_l.append(pp)
        ks_l.append(kn); vs_l.append(vn); ps_l.append(pn)
    y_prompt = rmsnorm(xp, final_norm_g)
    y_sample = rmsnorm(xs, final_norm_g)
    new_k_prompt = jnp.stack(kp_l, axis=0)
    new_v_prompt = jnp.stack(vp_l, axis=0)
    new_pool_prompt = jnp.stack(pp_l, axis=0)
    new_k_sample = jnp.stack(ks_l, axis=0)
    new_v_sample = jnp.stack(vs_l, axis=0)
    new_pool_sample = jnp.stack(ps_l, axis=0)
    return (y_prompt, y_sample, new_k_prompt, new_v_prompt, new_pool_prompt, new_k_sample, new_v_sample, new_pool_sample)
```

```python
import functools
import math

import jax
import jax.numpy as jnp
from jax import lax
from jax.experimental import pallas as pl
from jax.experimental.pallas import tpu as pltpu

F32 = jnp.float32
BF16 = jnp.bfloat16

D_MODEL = 1024
N_HEADS = 8
D_HEAD = 64
D_SB = N_HEADS * D_HEAD
POOL_WINDOWS = (2, 4, 8, 16)
D_POOL_GROUP = 128
D_POOL = len(POOL_WINDOWS) * D_POOL_GROUP
POOL_BUF = 15
POOL_HALO = 16
N_KEYS = 128
PEER_HEADS = 8
D_SUBKEY = 128
TOPK = 16
N_SEL = PEER_HEADS * TOPK
N_MOD = 6
EPS = 1e-6
ROW_SUBLANES = 8
LANES = 128
LOG_KEEP_FLOOR = -100.0
VMEM_LIMIT = 56 * 1024 * 1024


def _cparams(sem):
    return pltpu.CompilerParams(dimension_semantics=sem, vmem_limit_bytes=VMEM_LIMIT)


def _rms(x, g):
    return x * lax.rsqrt(jnp.mean(x * x, axis=-1, keepdims=True) + EPS) * g


def _dot(a, b):
    return jnp.dot(a, b, preferred_element_type=F32)


def _dot_nt(a, b):
    return lax.dot_general(a, b, (((1,), (1,)), ((), ())), preferred_element_type=F32)


def _split_bf16(x):
    hi = x.astype(BF16)
    lo = (x - hi.astype(F32)).astype(BF16)
    return hi, lo


def _ada_kernel(c_ref, w_ref, b_ref, o_ref):
    o_ref[...] = _dot(c_ref[...].astype(BF16), w_ref[...].astype(BF16)) + b_ref[...]


def _ada(c, w_ada, b_ada):
    n, d = c.shape
    nout = w_ada.shape[1]
    tn = 1024
    return pl.pallas_call(
        _ada_kernel,
        grid=(nout // tn,),
        in_specs=[pl.BlockSpec((n, d), lambda j: (0, 0)),
                  pl.BlockSpec((d, tn), lambda j: (0, j)),
                  pl.BlockSpec((1, tn), lambda j: (0, j))],
        out_specs=pl.BlockSpec((n, tn), lambda j: (0, j)),
        out_shape=jax.ShapeDtypeStruct((n, nout), F32),
        compiler_params=_cparams(("arbitrary",)),
        name="ada",
    )(c, w_ada, b_ada.reshape(1, nout))


def _qkvp_kernel(x_ref, mod_ref, g_ref, w_ref, q_ref, k_ref, v_ref, p_ref):
    x = x_ref[0]
    shift1 = mod_ref[0, 0:1, :]
    scale1 = mod_ref[0, 1:2, :]
    h = _rms(x, g_ref[...]) * (1.0 + scale1) + shift1
    proj = _dot(h.astype(BF16), w_ref[...])
    for hh in range(N_HEADS):
        q_ref[0, hh] = proj[:, hh * D_HEAD:(hh + 1) * D_HEAD]
        k_ref[0, hh] = proj[:, D_SB + hh * D_HEAD:D_SB + (hh + 1) * D_HEAD]
        v_ref[0, hh] = proj[:, 2 * D_SB + hh * D_HEAD:2 * D_SB + (hh + 1) * D_HEAD]
    p_ref[0] = proj[:, 3 * D_SB:3 * D_SB + D_POOL]


def _qkvp(x, mod, g_mix, w_qkvp, tT):
    B, T, D = x.shape
    nw = w_qkvp.shape[1]
    head_spec = pl.BlockSpec((1, N_HEADS, tT, D_HEAD), lambda b, t: (b, 0, t, 0))
    head_shape = jax.ShapeDtypeStruct((B, N_HEADS, T, D_HEAD), F32)
    return pl.pallas_call(
        _qkvp_kernel,
        grid=(B, T // tT),
        in_specs=[pl.BlockSpec((1, tT, D), lambda b, t: (b, t, 0)),
                  pl.BlockSpec((1, N_MOD, D), lambda b, t: (b, 0, 0)),
                  pl.BlockSpec((1, D), lambda b, t: (0, 0)),
                  pl.BlockSpec((D, nw), lambda b, t: (0, 0))],
        out_specs=[head_spec, head_spec, head_spec,
                   pl.BlockSpec((1, tT, D_POOL), lambda b, t: (b, t, 0))],
        out_shape=[head_shape, head_shape, head_shape,
                   jax.ShapeDtypeStruct((B, T, D_POOL), F32)],
        compiler_params=_cparams(("parallel", "arbitrary")),
        name="qkvp",
    )(x, mod, g_mix, w_qkvp)


def _attn_kernel(q_ref, kd_ref, vd_ref, kp_ref, vp_ref, o_ref, acc_ref, carry_ref, *, tq, tk, n_prev):
    q = (q_ref[0, 0] * (D_HEAD ** -0.5)).astype(BF16)

    def block(kb, vb, tkb, masked, first):
        z = _dot_nt(q, kb.astype(BF16))
        sp = jnp.maximum(z, 0.0) + jnp.log1p(jnp.exp(-jnp.abs(z)))
        log_keep = -sp
        log_beta = z - sp
        if masked:
            row = lax.broadcasted_iota(jnp.int32, (tq, tkb), 0)
            col = lax.broadcasted_iota(jnp.int32, (tq, tkb), 1)
            mask = col < row
            log_keep = jnp.where(mask, log_keep, 0.0)
        r2 = lax.broadcasted_iota(jnp.int32, (tkb, tkb), 0)
        c2 = lax.broadcasted_iota(jnp.int32, (tkb, tkb), 1)
        tri = jnp.where(r2 > c2, 1.0, 0.0).astype(BF16)
        hi, lo = _split_bf16(log_keep)
        suffix = _dot(hi, tri) + _dot(lo, tri)
        total = suffix[:, 0:1] + log_keep[:, 0:1]
        if first:
            a = jnp.exp(log_beta + suffix)
        else:
            a = jnp.exp(log_beta + suffix + carry_ref[...])
        if masked:
            a = jnp.where(mask, a, 0.0)
        contrib = _dot(a.astype(BF16), vb.astype(BF16))
        if first:
            acc_ref[...] = contrib
            carry_ref[...] = total
        else:
            acc_ref[...] += contrib
            carry_ref[...] += total

    block(kd_ref[0, 0], vd_ref[0, 0], tq, True, True)

    n_blocks = pl.program_id(2) if n_prev is None else jnp.int32(n_prev)

    def cond(c):
        j, mx = c
        return jnp.logical_and(j >= 0, mx > LOG_KEEP_FLOOR)

    def body(c):
        j, _ = c
        start = pl.multiple_of(j * tk, tk)
        block(kp_ref[0, 0, pl.ds(start, tk), :], vp_ref[0, 0, pl.ds(start, tk), :], tk, False, False)
        return j - 1, jnp.max(carry_ref[...])

    lax.while_loop(cond, body, (n_blocks - 1, jnp.max(carry_ref[...])))
    o_ref[0, 0] = acc_ref[...]


def _attention(q, k, v, past_k, past_v, tq, tk):
    B, H, T, dh = q.shape
    if past_k is None:
        assert tq == tk
        kp, vp, n_prev = k, v, None
    else:
        assert T == tq
        kp, vp, n_prev = past_k, past_v, past_k.shape[2] // tk
    Tp = kp.shape[2]
    tile = pl.BlockSpec((1, 1, tq, dh), lambda b, h, i: (b, h, i, 0))
    full = pl.BlockSpec((1, 1, Tp, dh), lambda b, h, i: (b, h, 0, 0))
    return pl.pallas_call(
        functools.partial(_attn_kernel, tq=tq, tk=tk, n_prev=n_prev),
        grid=(B, H, T // tq),
        in_specs=[tile, tile, tile, full, full],
        out_specs=tile,
        out_shape=jax.ShapeDtypeStruct((B, H, T, dh), F32),
        scratch_shapes=[pltpu.VMEM((tq, dh), F32), pltpu.VMEM((tq, 1), F32)],
        compiler_params=_cparams(("parallel", "parallel", "arbitrary")),
        name="attn",
    )(q, k, v, kp, vp)


def _mix_kernel(x_ref, mod_ref, gmix_ref, gffn_ref, o_ref, p_ref, halo_ref, pre_ref,
                wg_ref, wgrp_ref, pscale_ref, wsb_ref, wpu_ref, wout_ref, wq_ref, sk_ref,
                x1_ref, h2_ref, st_ref, *, tT, n_valid):
    t = pl.program_id(1)
    x = x_ref[0]
    shift1 = mod_ref[0, 0:1, :]
    scale1 = mod_ref[0, 1:2, :]
    gate1 = mod_ref[0, 2:3, :]
    shift2 = mod_ref[0, 3:4, :]
    scale2 = mod_ref[0, 4:5, :]
    h = _rms(x, gmix_ref[...]) * (1.0 + scale1) + shift1
    g = _dot(h.astype(BF16), wg_ref[...])

    y_sb = _dot(o_ref[0, 0].astype(BF16), wsb_ref[0])
    for hh in range(1, N_HEADS):
        y_sb += _dot(o_ref[0, hh].astype(BF16), wsb_ref[hh])

    p = p_ref[0]
    halo = jnp.where(t == 0, pre_ref[0], halo_ref[0])
    full = jnp.concatenate([halo, p], axis=0)
    pos = t * tT + lax.broadcasted_iota(jnp.int32, (tT, 1), 0)
    pooled = []
    for gi, w in enumerate(POOL_WINDOWS):
        sl = slice(gi * D_POOL_GROUP, (gi + 1) * D_POOL_GROUP)
        a = full[:, sl]
        sh = 1
        while sh < w:
            a = a + pltpu.roll(a, sh, axis=0)
            sh *= 2
        cnt = jnp.minimum(n_valid + pos + 1, w).astype(F32)
        pm = a[POOL_HALO:, :] / cnt - p[:, sl]
        pooled.append(_dot(pm.astype(BF16), wgrp_ref[gi]))
    pooled = jnp.concatenate(pooled, axis=1) * pscale_ref[...]
    y_pool = _dot(pooled.astype(BF16), wpu_ref[...])

    merged = jax.nn.sigmoid(g[:, :D_MODEL]) * y_sb + jax.nn.sigmoid(g[:, D_MODEL:]) * y_pool
    x1 = x + gate1 * _dot(merged.astype(BF16), wout_ref[...])
    x1_ref[0] = x1
    h2 = _rms(x1, gffn_ref[...]) * (1.0 + scale2) + shift2
    h2_ref[0] = h2
    qry = _dot(h2.astype(BF16), wq_ref[...])
    for hp in range(2 * PEER_HEADS):
        qs = qry[:, hp * D_SUBKEY:(hp + 1) * D_SUBKEY].astype(BF16)
        st_ref[0, hp] = _dot_nt(sk_ref[hp % 2], qs)


def _mix(x, mod, g_mix, g_ffn, o, p, prefix, n_valid, w_g, w_grp, pool_scale, w_sb, w_pu, w_out, w_q, sk, tT):
    B, T, D = x.shape
    hb = tT // POOL_HALO
    const = lambda shape: pl.BlockSpec(shape, lambda b, t: (0,) * len(shape))
    tok = lambda last: pl.BlockSpec((1, tT, last), lambda b, t: (b, t, 0))
    return pl.pallas_call(
        functools.partial(_mix_kernel, tT=tT, n_valid=n_valid),
        grid=(B, T // tT),
        in_specs=[tok(D),
                  pl.BlockSpec((1, N_MOD, D), lambda b, t: (b, 0, 0)),
                  const((1, D)), const((1, D)),
                  pl.BlockSpec((1, N_HEADS, tT, D_HEAD), lambda b, t: (b, 0, t, 0)),
                  tok(D_POOL),
                  pl.BlockSpec((1, POOL_HALO, D_POOL), lambda b, t: (b, jnp.maximum(t * hb - 1, 0), 0)),
                  pl.BlockSpec((1, POOL_HALO, D_POOL), lambda b, t: (b, 0, 0)),
                  const(w_g.shape), const(w_grp.shape), const((1, D_POOL)), const(w_sb.shape),
                  const(w_pu.shape), const(w_out.shape), const(w_q.shape), const(sk.shape)],
        out_specs=[tok(D), tok(D),
                   pl.BlockSpec((1, 2 * PEER_HEADS, N_KEYS, tT), lambda b, t: (b, 0, 0, t))],
        out_shape=[jax.ShapeDtypeStruct((B, T, D), F32), jax.ShapeDtypeStruct((B, T, D), F32),
                   jax.ShapeDtypeStruct((B, 2 * PEER_HEADS, N_KEYS, T), F32)],
        compiler_params=_cparams(("parallel", "arbitrary")),
        name="mix",
    )(x, mod, g_mix, g_ffn, o, p, p, prefix, w_g, w_grp, pool_scale, w_sb, w_pu, w_out, w_q, sk)


def _take_top(s, payload, n):
    rows = s.shape[0]
    row = lax.broadcasted_iota(jnp.int32, s.shape, 0)
    vals, pays = [], []
    for _ in range(n):
        m = jnp.max(s, axis=0, keepdims=True)
        pick = jnp.min(jnp.where(s == m, row, rows), axis=0, keepdims=True)
        hit = row == pick
        vals.append(m)
        pays.append(jnp.max(jnp.where(hit, payload, -1), axis=0, keepdims=True))
        s = jnp.where(hit, -jnp.inf, s)
    return jnp.concatenate(vals, axis=0), jnp.concatenate(pays, axis=0)


def _topk_kernel(st_ref, idx_ref, gate_ref):
    tN = st_ref.shape[-1]
    key_id = lax.broadcasted_iota(jnp.int32, (N_KEYS, tN), 0)
    for hh in range(PEER_HEADS):
        s1, i1 = _take_top(st_ref[0, 2 * hh], key_id, TOPK)
        s2, i2 = _take_top(st_ref[0, 2 * hh + 1], key_id, TOPK)
        cand_s = jnp.concatenate([s1[a:a + 1] + s2 for a in range(TOPK)], axis=0)
        cand_i = jnp.concatenate([i1[a:a + 1] * N_KEYS + i2 for a in range(TOPK)], axis=0)
        top_s, top_i = _take_top(cand_s, cand_i, TOPK)
        e = jnp.exp(top_s - top_s[0:1])
        idx_ref[0, hh] = top_i
        gate_ref[0, hh] = e / jnp.sum(e, axis=0, keepdims=True)


def _topk(st, tN):
    B, _, _, T = st.shape
    out_spec = pl.BlockSpec((1, PEER_HEADS, TOPK, tN), lambda b, t: (b, 0, 0, t))
    return pl.pallas_call(
        _topk_kernel,
        grid=(B, T // tN),
        in_specs=[pl.BlockSpec((1, 2 * PEER_HEADS, N_KEYS, tN), lambda b, t: (b, 0, 0, t))],
        out_specs=[out_spec, out_spec],
        out_shape=[jax.ShapeDtypeStruct((B, PEER_HEADS, TOPK, T), jnp.int32),
                   jax.ShapeDtypeStruct((B, PEER_HEADS, TOPK, T), F32)],
        compiler_params=_cparams(("parallel", "arbitrary")),
        name="topk",
    )(st)


def _peer_kernel(idx_ref, gate_ref, h_ref, u_hbm, v_hbm, o_ref, ub0, ub1, vb0, vb1, sems, *, tT):
    ubufs, vbufs = (ub0, ub1), (vb0, vb1)
    rows = N_SEL * ROW_SUBLANES
    r_id = lax.broadcasted_iota(jnp.int32, (ROW_SUBLANES, rows), 1)
    s_id = lax.broadcasted_iota(jnp.int32, (ROW_SUBLANES, rows), 0)
    sub = (r_id % ROW_SUBLANES) == s_id
    group = jnp.where(lax.broadcasted_iota(jnp.int32, (N_SEL, rows), 1) // ROW_SUBLANES
                      == lax.broadcasted_iota(jnp.int32, (N_SEL, rows), 0), 1.0, 0.0).astype(BF16)
    group_t = jnp.where(lax.broadcasted_iota(jnp.int32, (rows, N_SEL), 0) // ROW_SUBLANES
                        == lax.broadcasted_iota(jnp.int32, (rows, N_SEL), 1), 1.0, 0.0).astype(BF16)

    def start(t, slot):
        for k in range(N_SEL):
            e = idx_ref[t, k]
            pltpu.make_async_copy(u_hbm.at[e], ubufs[slot].at[k], sems.at[0, slot]).start()
            pltpu.make_async_copy(v_hbm.at[e], vbufs[slot].at[k], sems.at[1, slot]).start()

    def wait(slot):
        pltpu.make_async_copy(u_hbm.at[pl.ds(0, N_SEL)], ubufs[slot], sems.at[0, slot]).wait()
        pltpu.make_async_copy(v_hbm.at[pl.ds(0, N_SEL)], vbufs[slot], sems.at[1, slot]).wait()

    def compute(t, slot):
        xt = h_ref[t].astype(BF16)
        u2 = ubufs[slot][...].reshape(rows, LANES).astype(BF16)
        y = jnp.where(sub, _dot_nt(xt, u2), 0.0)
        ysum = jnp.sum(y, axis=0, keepdims=True)
        hi, lo = _split_bf16(ysum)
        act = _dot(hi, group_t) + _dot(lo, group_t)
        gelu = 0.5 * act * (1.0 + lax.erf(act * (2.0 ** -0.5)))
        wgt = gate_ref[pl.ds(t, 1), :] * gelu
        hi, lo = _split_bf16(wgt)
        wrow = _dot(hi, group) + _dot(lo, group)
        wmat = jnp.where(sub, jnp.broadcast_to(wrow, (ROW_SUBLANES, rows)), 0.0)
        hi, lo = _split_bf16(wmat)
        v2 = vbufs[slot][...].reshape(rows, LANES).astype(BF16)
        o_ref[t] = _dot(hi, v2) + _dot(lo, v2)

    start(0, 0)

    def pair(i, c):
        t0 = 2 * i
        start(t0 + 1, 1)
        wait(0)
        compute(t0, 0)

        @pl.when(t0 + 2 < tT)
        def _():
            start(t0 + 2, 0)

        wait(1)
        compute(t0 + 1, 1)
        return c

    lax.fori_loop(0, tT // 2, pair, 0)


def _peer(idx, gate, h2, u_tab, v_tab, tT):
    N = idx.shape[0]
    buf = pltpu.VMEM((N_SEL, ROW_SUBLANES, LANES), F32)
    return pl.pallas_call(
        functools.partial(_peer_kernel, tT=tT),
        grid=(N // tT,),
        in_specs=[pl.BlockSpec((tT, N_SEL), lambda i: (i, 0), memory_space=pltpu.SMEM),
                  pl.BlockSpec((tT, N_SEL), lambda i: (i, 0)),
                  pl.BlockSpec((tT, ROW_SUBLANES, LANES), lambda i: (i, 0, 0)),
                  pl.BlockSpec(memory_space=pl.ANY),
                  pl.BlockSpec(memory_space=pl.ANY)],
        out_specs=pl.BlockSpec((tT, ROW_SUBLANES, LANES), lambda i: (i, 0, 0)),
        out_shape=jax.ShapeDtypeStruct((N, ROW_SUBLANES, LANES), F32),
        scratch_shapes=[buf, buf, buf, buf, pltpu.SemaphoreType.DMA((2, 2))],
        compiler_params=_cparams(("arbitrary",)),
        name="peer",
    )(idx, gate, h2, u_tab, v_tab)


def _final_kernel(x1_ref, peer_ref, mod_ref, g_ref, y_ref):
    gate2 = mod_ref[0, 5:6, :]
    y_ref[0] = _rms(x1_ref[0] + gate2 * peer_ref[0], g_ref[...])


def _final(x1, peer_out, mod, g_final, tT):
    B, T, D = x1.shape
    tok = pl.BlockSpec((1, tT, D), lambda b, t: (b, t, 0))
    return pl.pallas_call(
        _final_kernel,
        grid=(B, T // tT),
        in_specs=[tok, tok, pl.BlockSpec((1, N_MOD, D), lambda b, t: (b, 0, 0)),
                  pl.BlockSpec((1, D), lambda b, t: (0, 0))],
        out_specs=tok,
        out_shape=jax.ShapeDtypeStruct((B, T, D), F32),
        compiler_params=_cparams(("parallel", "arbitrary")),
        name="final",
    )(x1, peer_out, mod, g_final)


def _group(x, mod, past_k, past_v, prefix, n_valid, wts, tiles):
    B, T, D = x.shape
    q, k, v, p = _qkvp(x, mod, wts["g_mix"], wts["w_qkvp"], tiles["proj"])
    o = _attention(q, k, v, past_k, past_v, tiles["tq"], tiles["tk"])
    x1, h2, st = _mix(x, mod, wts["g_mix"], wts["g_ffn"], o, p, prefix, n_valid,
                      wts["w_g"], wts["w_grp"], wts["pool_scale"], wts["w_sb"], wts["w_pu"],
                      wts["w_out"], wts["w_q"], wts["sk"], tiles["mix"])
    idx, gate = _topk(st, tiles["topk"])
    to_rows = lambda a: a.transpose(0, 3, 1, 2).reshape(B * T, N_SEL)
    peer_out = _peer(to_rows(idx), to_rows(gate), h2.reshape(B * T, ROW_SUBLANES, LANES),
                     wts["u_tab"], wts["v_tab"], tiles["peer"])
    y = _final(x1, peer_out.reshape(B, T, D), mod, wts["g_final"], tiles["mix"])
    new_pool = jnp.concatenate([prefix[:, 1:], p], axis=1)[:, -POOL_BUF:]
    return y, k, v, new_pool


def kernel(x_prompt, x_sample, cache_sb_k, cache_sb_v, state_pool, c_prompt, c_sample, w_ada, b_ada, norm_mix_g, norm_ffn_g, w_in, w_pool_grp, pool_scale, w_sb_up, w_pool_up, w_out, peer_w_query, peer_sub_keys, peer_u, peer_v, final_norm_g):
    depth = w_ada.shape[0]
    assert depth == 1
    B, T, D = x_prompt.shape
    Bs, Ts, _ = x_sample.shape
    l = 0
    n_qkvp = 3 * D_SB + D_POOL
    wts = dict(
        g_mix=norm_mix_g[l].reshape(1, D), g_ffn=norm_ffn_g[l].reshape(1, D), g_final=final_norm_g.reshape(1, D),
        w_qkvp=w_in[l][:, :n_qkvp].astype(BF16), w_g=w_in[l][:, n_qkvp:].astype(BF16),
        w_grp=w_pool_grp[l].astype(BF16), pool_scale=pool_scale[l].reshape(1, D_POOL),
        w_sb=w_sb_up[l].astype(BF16).reshape(N_HEADS, D_HEAD, D), w_pu=w_pool_up[l].astype(BF16),
        w_out=w_out[l].astype(BF16), w_q=peer_w_query[l].astype(BF16), sk=peer_sub_keys[l].astype(BF16),
        u_tab=peer_u[l].reshape(-1, ROW_SUBLANES, LANES), v_tab=peer_v[l].reshape(-1, ROW_SUBLANES, LANES),
    )
    mod = _ada(jnp.concatenate([c_prompt, c_sample], axis=0), w_ada[l], b_ada[l]).reshape(B + Bs, N_MOD, D)

    zero_prefix = jnp.zeros((B, POOL_HALO, D_POOL), F32)
    tiles_p = dict(proj=min(T, 512), tq=min(T, 256), tk=min(T, 256), mix=min(T, 256), topk=min(T, 256), peer=16)
    yp, kp, vp, pp = _group(x_prompt, mod[:B], None, None, zero_prefix, 0, wts, tiles_p)

    prefix_s = jnp.concatenate([jnp.zeros((Bs, POOL_HALO - POOL_BUF, D_POOL), F32), state_pool[l]], axis=1)
    tiles_s = dict(proj=Ts, tq=Ts, tk=min(cache_sb_k.shape[3], 256), mix=Ts, topk=Ts, peer=16)
    ys, ks, vs, ps = _group(x_sample, mod[B:], cache_sb_k[l], cache_sb_v[l], prefix_s, POOL_BUF, wts, tiles_s)

    return (yp, ys, kp[None], vp[None], pp[None], ks[None], vs[None], ps[None])
```

```python
import functools
import math

import jax
import jax.numpy as jnp
from jax import lax
from jax.experimental import pallas as pl
from jax.experimental.pallas import tpu as pltpu

F32 = jnp.float32
BF16 = jnp.bfloat16

D_MODEL = 1024
N_HEADS = 8
D_HEAD = 64
D_SB = N_HEADS * D_HEAD
POOL_WINDOWS = (2, 4, 8, 16)
D_POOL_GROUP = 128
D_POOL = len(POOL_WINDOWS) * D_POOL_GROUP
POOL_BUF = 15
POOL_HALO = 16
N_KEYS = 128
PEER_HEADS = 8
D_SUBKEY = 128
TOPK = 16
N_SEL = PEER_HEADS * TOPK
N_MOD = 6
EPS = 1e-6
ROW_SUBLANES = 8
LANES = 128
LOG_KEEP_FLOOR = -100.0
VMEM_LIMIT = 56 * 1024 * 1024


def _cparams(sem):
    return pltpu.CompilerParams(dimension_semantics=sem, vmem_limit_bytes=VMEM_LIMIT)


def _rms(x, g):
    return x * lax.rsqrt(jnp.mean(x * x, axis=-1, keepdims=True) + EPS) * g


def _dot(a, b):
    return jnp.dot(a, b, preferred_element_type=F32)


def _dot_nt(a, b):
    return lax.dot_general(a, b, (((1,), (1,)), ((), ())), preferred_element_type=F32)


def _split_bf16(x):
    hi = x.astype(BF16)
    lo = (x - hi.astype(F32)).astype(BF16)
    return hi, lo


def _ada_kernel(c_ref, w_ref, b_ref, o_ref):
    o_ref[...] = _dot(c_ref[...].astype(BF16), w_ref[...].astype(BF16)) + b_ref[...]


def _ada(c, w_ada, b_ada):
    n, d = c.shape
    nout = w_ada.shape[1]
    tn = 1024
    return pl.pallas_call(
        _ada_kernel,
        grid=(nout // tn,),
        in_specs=[pl.BlockSpec((n, d), lambda j: (0, 0)),
                  pl.BlockSpec((d, tn), lambda j: (0, j)),
                  pl.BlockSpec((1, tn), lambda j: (0, j))],
        out_specs=pl.BlockSpec((n, tn), lambda j: (0, j)),
        out_shape=jax.ShapeDtypeStruct((n, nout), F32),
        compiler_params=_cparams(("arbitrary",)),
        name="ada",
    )(c, w_ada, b_ada.reshape(1, nout))


def _qkvp_kernel(x_ref, mod_ref, g_ref, w_ref, q_ref, k_ref, v_ref, p_ref):
    x = x_ref[0]
    shift1 = mod_ref[0, 0:1, :]
    scale1 = mod_ref[0, 1:2, :]
    h = _rms(x, g_ref[...]) * (1.0 + scale1) + shift1
    proj = _dot(h.astype(BF16), w_ref[...])
    for hh in range(N_HEADS):
        q_ref[0, hh] = proj[:, hh * D_HEAD:(hh + 1) * D_HEAD]
        k_ref[0, hh] = proj[:, D_SB + hh * D_HEAD:D_SB + (hh + 1) * D_HEAD]
        v_ref[0, hh] = proj[:, 2 * D_SB + hh * D_HEAD:2 * D_SB + (hh + 1) * D_HEAD]
    p_ref[0] = proj[:, 3 * D_SB:3 * D_SB + D_POOL]


def _qkvp(x, mod, g_mix, w_qkvp, tT):
    B, T, D = x.shape
    nw = w_qkvp.shape[1]
    head_spec = pl.BlockSpec((1, N_HEADS, tT, D_HEAD), lambda b, t: (b, 0, t, 0))
    head_shape = jax.ShapeDtypeStruct((B, N_HEADS, T, D_HEAD), F32)
    return pl.pallas_call(
        _qkvp_kernel,
        grid=(B, T // tT),
        in_specs=[pl.BlockSpec((1, tT, D), lambda b, t: (b, t, 0)),
                  pl.BlockSpec((1, N_MOD, D), lambda b, t: (b, 0, 0)),
                  pl.BlockSpec((1, D), lambda b, t: (0, 0)),
                  pl.BlockSpec((D, nw), lambda b, t: (0, 0))],
        out_specs=[head_spec, head_spec, head_spec,
                   pl.BlockSpec((1, tT, D_POOL), lambda b, t: (b, t, 0))],
        out_shape=[head_shape, head_shape, head_shape,
                   jax.ShapeDtypeStruct((B, T, D_POOL), F32)],
        compiler_params=_cparams(("parallel", "arbitrary")),
        name="qkvp",
    )(x, mod, g_mix, w_qkvp)


def _attn_kernel(q_ref, kd_ref, vd_ref, kp_ref, vp_ref, o_ref, acc_ref, carry_ref, *, tq, tk, n_prev):
    q = (q_ref[0, 0] * (D_HEAD ** -0.5)).astype(BF16)

    def block(kb, vb, tkb, masked, first):
        z = _dot_nt(q, kb.astype(BF16))
        sp = jnp.maximum(z, 0.0) + jnp.log1p(jnp.exp(-jnp.abs(z)))
        log_keep = -sp
        log_beta = z - sp
        if masked:
            row = lax.broadcasted_iota(jnp.int32, (tq, tkb), 0)
            col = lax.broadcasted_iota(jnp.int32, (tq, tkb), 1)
            mask = col < row
            log_keep = jnp.where(mask, log_keep, 0.0)
        r2 = lax.broadcasted_iota(jnp.int32, (tkb, tkb), 0)
        c2 = lax.broadcasted_iota(jnp.int32, (tkb, tkb), 1)
        tri = jnp.where(r2 > c2, 1.0, 0.0).astype(BF16)
        hi, lo = _split_bf16(log_keep)
        suffix = _dot(hi, tri) + _dot(lo, tri)
        total = suffix[:, 0:1] + log_keep[:, 0:1]
        if first:
            a = jnp.exp(log_beta + suffix)
        else:
            a = jnp.exp(log_beta + suffix + carry_ref[...])
        if masked:
            a = jnp.where(mask, a, 0.0)
        contrib = _dot(a.astype(BF16), vb.astype(BF16))
        if first:
            acc_ref[...] = contrib
            carry_ref[...] = total
        else:
            acc_ref[...] += contrib
            carry_ref[...] += total

    block(kd_ref[0, 0], vd_ref[0, 0], tq, True, True)

    n_blocks = pl.program_id(2) if n_prev is None else jnp.int32(n_prev)

    def cond(c):
        j, mx = c
        return jnp.logical_and(j >= 0, mx > LOG_KEEP_FLOOR)

    def body(c):
        j, _ = c
        start = pl.multiple_of(j * tk, tk)
        block(kp_ref[0, 0, pl.ds(start, tk), :], vp_ref[0, 0, pl.ds(start, tk), :], tk, False, False)
        return j - 1, jnp.max(carry_ref[...])

    lax.while_loop(cond, body, (n_blocks - 1, jnp.max(carry_ref[...])))
    o_ref[0, 0] = acc_ref[...]


def _attention(q, k, v, past_k, past_v, tq, tk):
    B, H, T, dh = q.shape
    if past_k is None:
        assert tq == tk
        kp, vp, n_prev = k, v, None
    else:
        assert T == tq
        kp, vp, n_prev = past_k, past_v, past_k.shape[2] // tk
    Tp = kp.shape[2]
    tile = pl.BlockSpec((1, 1, tq, dh), lambda b, h, i: (b, h, i, 0))
    full = pl.BlockSpec((1, 1, Tp, dh), lambda b, h, i: (b, h, 0, 0))
    return pl.pallas_call(
        functools.partial(_attn_kernel, tq=tq, tk=tk, n_prev=n_prev),
        grid=(B, H, T // tq),
        in_specs=[tile, tile, tile, full, full],
        out_specs=tile,
        out_shape=jax.ShapeDtypeStruct((B, H, T, dh), F32),
        scratch_shapes=[pltpu.VMEM((tq, dh), F32), pltpu.VMEM((tq, 1), F32)],
        compiler_params=_cparams(("parallel", "parallel", "arbitrary")),
        name="attn",
    )(q, k, v, kp, vp)


def _mix_kernel(x_ref, mod_ref, gmix_ref, gffn_ref, o_ref, p_ref, halo_ref, pre_ref,
                wg_ref, wgrp_ref, pscale_ref, wsb_ref, wpu_ref, wout_ref, wq_ref, sk_ref,
                x1_ref, h2_ref, st_ref, *, tT, n_valid):
    t = pl.program_id(1)
    x = x_ref[0]
    shift1 = mod_ref[0, 0:1, :]
    scale1 = mod_ref[0, 1:2, :]
    gate1 = mod_ref[0, 2:3, :]
    shift2 = mod_ref[0, 3:4, :]
    scale2 = mod_ref[0, 4:5, :]
    h = _rms(x, gmix_ref[...]) * (1.0 + scale1) + shift1
    g = _dot(h.astype(BF16), wg_ref[...])

    y_sb = _dot(o_ref[0, 0].astype(BF16), wsb_ref[0])
    for hh in range(1, N_HEADS):
        y_sb += _dot(o_ref[0, hh].astype(BF16), wsb_ref[hh])

    p = p_ref[0]
    halo = jnp.where(t == 0, pre_ref[0], halo_ref[0])
    full = jnp.concatenate([halo, p], axis=0)
    pos = t * tT + lax.broadcasted_iota(jnp.int32, (tT, 1), 0)
    pooled = []
    for gi, w in enumerate(POOL_WINDOWS):
        sl = slice(gi * D_POOL_GROUP, (gi + 1) * D_POOL_GROUP)
        a = full[:, sl]
        sh = 1
        while sh < w:
            a = a + pltpu.roll(a, sh, axis=0)
            sh *= 2
        cnt = jnp.minimum(n_valid + pos + 1, w).astype(F32)
        pm = a[POOL_HALO:, :] / cnt - p[:, sl]
        pooled.append(_dot(pm.astype(BF16), wgrp_ref[gi]))
    pooled = jnp.concatenate(pooled, axis=1) * pscale_ref[...]
    y_pool = _dot(pooled.astype(BF16), wpu_ref[...])

    merged = jax.nn.sigmoid(g[:, :D_MODEL]) * y_sb + jax.nn.sigmoid(g[:, D_MODEL:]) * y_pool
    x1 = x + gate1 * _dot(merged.astype(BF16), wout_ref[...])
    x1_ref[0] = x1
    h2 = _rms(x1, gffn_ref[...]) * (1.0 + scale2) + shift2
    h2_ref[0] = h2
    qry = _dot(h2.astype(BF16), wq_ref[...])
    for hp in range(2 * PEER_HEADS):
        qs = qry[:, hp * D_SUBKEY:(hp + 1) * D_SUBKEY].astype(BF16)
        st_ref[0, hp] = _dot_nt(sk_ref[hp % 2], qs)


def _mix(x, mod, g_mix, g_ffn, o, p, prefix, n_valid, w_g, w_grp, pool_scale, w_sb, w_pu, w_out, w_q, sk, tT):
    B, T, D = x.shape
    hb = tT // POOL_HALO
    const = lambda shape: pl.BlockSpec(shape, lambda b, t: (0,) * len(shape))
    tok = lambda last: pl.BlockSpec((1, tT, last), lambda b, t: (b, t, 0))
    return pl.pallas_call(
        functools.partial(_mix_kernel, tT=tT, n_valid=n_valid),
        grid=(B, T // tT),
        in_specs=[tok(D),
                  pl.BlockSpec((1, N_MOD, D), lambda b, t: (b, 0, 0)),
                  const((1, D)), const((1, D)),
                  pl.BlockSpec((1, N_HEADS, tT, D_HEAD), lambda b, t: (b, 0, t, 0)),
                  tok(D_POOL),
                  pl.BlockSpec((1, POOL_HALO, D_POOL), lambda b, t: (b, jnp.maximum(t * hb - 1, 0), 0)),
                  pl.BlockSpec((1, POOL_HALO, D_POOL), lambda b, t: (b, 0, 0)),
                  const(w_g.shape), const(w_grp.shape), const((1, D_POOL)), const(w_sb.shape),
                  const(w_pu.shape), const(w_out.shape), const(w_q.shape), const(sk.shape)],
        out_specs=[tok(D), tok(D),
                   pl.BlockSpec((1, 2 * PEER_HEADS, N_KEYS, tT), lambda b, t: (b, 0, 0, t))],
        out_shape=[jax.ShapeDtypeStruct((B, T, D), F32), jax.ShapeDtypeStruct((B, T, D), F32),
                   jax.ShapeDtypeStruct((B, 2 * PEER_HEADS, N_KEYS, T), F32)],
        compiler_params=_cparams(("parallel", "arbitrary")),
        name="mix",
    )(x, mod, g_mix, g_ffn, o, p, p, prefix, w_g, w_grp, pool_scale, w_sb, w_pu, w_out, w_q, sk)


def _take_top(s, n, payload=None):
    rows = s.shape[0]
    row = lax.broadcasted_iota(jnp.int32, s.shape, 0)
    vals, pays = [], []
    for _ in range(n):
        m = jnp.max(s, axis=0, keepdims=True)
        pick = jnp.min(jnp.where(s == m, row, rows), axis=0, keepdims=True)
        hit = row == pick
        vals.append(m)
        pays.append(pick if payload is None else jnp.max(jnp.where(hit, payload, -1), axis=0, keepdims=True))
        s = jnp.where(hit, -jnp.inf, s)
    return jnp.concatenate(vals, axis=0), jnp.concatenate(pays, axis=0)


def _candidate_pieces():
    pieces, a = [], 0
    while a < TOPK:
        nb = TOPK // (a + 1)
        if nb == 1:
            pieces.append((slice(a, TOPK), slice(0, 1)))
            break
        pieces.append((slice(a, a + 1), slice(0, min(TOPK, -(-nb // 8) * 8))))
        a += 1
    return pieces


def _topk_kernel(st_ref, idx_ref, gate_ref):
    pieces = _candidate_pieces()
    for hh in range(PEER_HEADS):
        s1, i1 = _take_top(st_ref[0, 2 * hh], TOPK)
        s2, i2 = _take_top(st_ref[0, 2 * hh + 1], TOPK)
        cand_s = jnp.concatenate([s1[sa] + s2[sb] for sa, sb in pieces], axis=0)
        cand_i = jnp.concatenate([i1[sa] * N_KEYS + i2[sb] for sa, sb in pieces], axis=0)
        top_s, top_i = _take_top(cand_s, TOPK, cand_i)
        e = jnp.exp(top_s - top_s[0:1])
        idx_ref[0, hh] = top_i
        gate_ref[0, hh] = e / jnp.sum(e, axis=0, keepdims=True)


def _topk(st, tN):
    B, _, _, T = st.shape
    out_spec = pl.BlockSpec((1, PEER_HEADS, TOPK, tN), lambda b, t: (b, 0, 0, t))
    return pl.pallas_call(
        _topk_kernel,
        grid=(B, T // tN),
        in_specs=[pl.BlockSpec((1, 2 * PEER_HEADS, N_KEYS, tN), lambda b, t: (b, 0, 0, t))],
        out_specs=[out_spec, out_spec],
        out_shape=[jax.ShapeDtypeStruct((B, PEER_HEADS, TOPK, T), jnp.int32),
                   jax.ShapeDtypeStruct((B, PEER_HEADS, TOPK, T), F32)],
        compiler_params=_cparams(("parallel", "arbitrary")),
        name="topk",
    )(st)


PACKED_SUBLANES = 8
BF16_ROWS = 2 * PACKED_SUBLANES
DMA_QUEUES = 2


def _pack_bf16_rows(t):
    e = t.shape[0]
    pairs = t.astype(BF16).reshape(e, ROW_SUBLANES // 2, 2, LANES).transpose(0, 1, 3, 2)
    return lax.bitcast_convert_type(pairs, jnp.int32)


def _peer_kernel(idx_ref, gate_ref, h_ref, tab_hbm, o_ref, buf0, buf1, sems, *, tT):
    bufs = (buf0, buf1)
    rows = N_SEL * BF16_ROWS
    r_id = lax.broadcasted_iota(jnp.int32, (ROW_SUBLANES, rows), 1) % BF16_ROWS
    s_id = lax.broadcasted_iota(jnp.int32, (ROW_SUBLANES, rows), 0)
    sub_u = r_id == s_id
    sub_v = r_id == s_id + ROW_SUBLANES
    group = jnp.where(lax.broadcasted_iota(jnp.int32, (N_SEL, rows), 1) // BF16_ROWS
                      == lax.broadcasted_iota(jnp.int32, (N_SEL, rows), 0), 1.0, 0.0).astype(BF16)
    group_t = jnp.where(lax.broadcasted_iota(jnp.int32, (rows, N_SEL), 0) // BF16_ROWS
                        == lax.broadcasted_iota(jnp.int32, (rows, N_SEL), 1), 1.0, 0.0).astype(BF16)

    def start(t, slot):
        for k in range(N_SEL):
            pltpu.make_async_copy(tab_hbm.at[idx_ref[t, k]], bufs[slot].at[k], sems.at[slot]).start(
                priority=k % DMA_QUEUES)

    def wait(slot):
        pltpu.make_async_copy(tab_hbm.at[pl.ds(0, N_SEL)], bufs[slot], sems.at[slot]).wait()

    def compute(t, slot):
        xt = h_ref[t].astype(BF16)
        words = bufs[slot][...].reshape(N_SEL * PACKED_SUBLANES, LANES)
        uv = pltpu.bitcast(words, BF16)
        y = jnp.where(sub_u, _dot_nt(xt, uv), 0.0)
        ysum = jnp.sum(y, axis=0, keepdims=True)
        hi, lo = _split_bf16(ysum)
        act = _dot(hi, group_t) + _dot(lo, group_t)
        gelu = 0.5 * act * (1.0 + lax.erf(act * (2.0 ** -0.5)))
        wgt = gate_ref[pl.ds(t, 1), :] * gelu
        hi, lo = _split_bf16(wgt)
        wrow = _dot(hi, group) + _dot(lo, group)
        wmat = jnp.where(sub_v, jnp.broadcast_to(wrow, (ROW_SUBLANES, rows)), 0.0)
        hi, lo = _split_bf16(wmat)
        o_ref[t] = _dot(hi, uv) + _dot(lo, uv)

    start(0, 0)

    def pair(i, c):
        t0 = 2 * i
        start(t0 + 1, 1)
        wait(0)
        compute(t0, 0)

        @pl.when(t0 + 2 < tT)
        def _():
            start(t0 + 2, 0)

        wait(1)
        compute(t0 + 1, 1)
        return c

    lax.fori_loop(0, tT // 2, pair, 0)


def _peer(idx, gate, h2, table, tT):
    N = idx.shape[0]
    buf = pltpu.VMEM((N_SEL, PACKED_SUBLANES, LANES), jnp.int32)
    return pl.pallas_call(
        functools.partial(_peer_kernel, tT=tT),
        grid=(N // tT,),
        in_specs=[pl.BlockSpec((tT, N_SEL), lambda i: (i, 0), memory_space=pltpu.SMEM),
                  pl.BlockSpec((tT, N_SEL), lambda i: (i, 0)),
                  pl.BlockSpec((tT, ROW_SUBLANES, LANES), lambda i: (i, 0, 0)),
                  pl.BlockSpec(memory_space=pl.ANY)],
        out_specs=pl.BlockSpec((tT, ROW_SUBLANES, LANES), lambda i: (i, 0, 0)),
        out_shape=jax.ShapeDtypeStruct((N, ROW_SUBLANES, LANES), F32),
        scratch_shapes=[buf, buf, pltpu.SemaphoreType.DMA((2,))],
        compiler_params=_cparams(("arbitrary",)),
        name="peer",
    )(idx, gate, h2, table)


def _final_kernel(x1_ref, peer_ref, mod_ref, g_ref, y_ref):
    gate2 = mod_ref[0, 5:6, :]
    y_ref[0] = _rms(x1_ref[0] + gate2 * peer_ref[0], g_ref[...])


def _final(x1, peer_out, mod, g_final, tT):
    B, T, D = x1.shape
    tok = pl.BlockSpec((1, tT, D), lambda b, t: (b, t, 0))
    return pl.pallas_call(
        _final_kernel,
        grid=(B, T // tT),
        in_specs=[tok, tok, pl.BlockSpec((1, N_MOD, D), lambda b, t: (b, 0, 0)),
                  pl.BlockSpec((1, D), lambda b, t: (0, 0))],
        out_specs=tok,
        out_shape=jax.ShapeDtypeStruct((B, T, D), F32),
        compiler_params=_cparams(("parallel", "arbitrary")),
        name="final",
    )(x1, peer_out, mod, g_final)


def _group(x, mod, past_k, past_v, prefix, n_valid, wts, tiles):
    B, T, D = x.shape
    q, k, v, p = _qkvp(x, mod, wts["g_mix"], wts["w_qkvp"], tiles["proj"])
    o = _attention(q, k, v, past_k, past_v, tiles["tq"], tiles["tk"])
    x1, h2, st = _mix(x, mod, wts["g_mix"], wts["g_ffn"], o, p, prefix, n_valid,
                      wts["w_g"], wts["w_grp"], wts["pool_scale"], wts["w_sb"], wts["w_pu"],
                      wts["w_out"], wts["w_q"], wts["sk"], tiles["mix"])
    idx, gate = _topk(st, tiles["topk"])
    to_rows = lambda a: a.transpose(0, 3, 1, 2).reshape(B * T, N_SEL)
    peer_out = _peer(to_rows(idx), to_rows(gate), h2.reshape(B * T, ROW_SUBLANES, LANES),
                     wts["table"], tiles["peer"])
    y = _final(x1, peer_out.reshape(B, T, D), mod, wts["g_final"], tiles["mix"])
    new_pool = jnp.concatenate([prefix[:, 1:], p], axis=1)[:, -POOL_BUF:]
    return y, k, v, new_pool


def kernel(x_prompt, x_sample, cache_sb_k, cache_sb_v, state_pool, c_prompt, c_sample, w_ada, b_ada, norm_mix_g, norm_ffn_g, w_in, w_pool_grp, pool_scale, w_sb_up, w_pool_up, w_out, peer_w_query, peer_sub_keys, peer_u, peer_v, final_norm_g):
    depth = w_ada.shape[0]
    assert depth == 1
    B, T, D = x_prompt.shape
    Bs, Ts, _ = x_sample.shape
    l = 0
    n_qkvp = 3 * D_SB + D_POOL
    wts = dict(
        g_mix=norm_mix_g[l].reshape(1, D), g_ffn=norm_ffn_g[l].reshape(1, D), g_final=final_norm_g.reshape(1, D),
        w_qkvp=w_in[l][:, :n_qkvp].astype(BF16), w_g=w_in[l][:, n_qkvp:].astype(BF16),
        w_grp=w_pool_grp[l].astype(BF16), pool_scale=pool_scale[l].reshape(1, D_POOL),
        w_sb=w_sb_up[l].astype(BF16).reshape(N_HEADS, D_HEAD, D), w_pu=w_pool_up[l].astype(BF16),
        w_out=w_out[l].astype(BF16), w_q=peer_w_query[l].astype(BF16), sk=peer_sub_keys[l].astype(BF16),
        table=jnp.concatenate([_pack_bf16_rows(peer_u[l]), _pack_bf16_rows(peer_v[l])], axis=1),
    )
    mod = _ada(jnp.concatenate([c_prompt, c_sample], axis=0), w_ada[l], b_ada[l]).reshape(B + Bs, N_MOD, D)

    zero_prefix = jnp.zeros((B, POOL_HALO, D_POOL), F32)
    tiles_p = dict(proj=min(T, 512), tq=min(T, 256), tk=min(T, 256), mix=min(T, 256), topk=min(T, 256), peer=16)
    yp, kp, vp, pp = _group(x_prompt, mod[:B], None, None, zero_prefix, 0, wts, tiles_p)

    prefix_s = jnp.concatenate([jnp.zeros((Bs, POOL_HALO - POOL_BUF, D_POOL), F32), state_pool[l]], axis=1)
    tiles_s = dict(proj=Ts, tq=Ts, tk=min(cache_sb_k.shape[3], 256), mix=Ts, topk=Ts, peer=16)
    ys, ks, vs, ps = _group(x_sample, mod[B:], cache_sb_k[l], cache_sb_v[l], prefix_s, POOL_BUF, wts, tiles_s)

    return (yp, ys, kp[None], vp[None], pp[None], ks[None], vs[None], ps[None])
```

```python
import functools
import math

import jax
import jax.numpy as jnp
from jax import lax
from jax.experimental import pallas as pl
from jax.experimental.pallas import tpu as pltpu

F32 = jnp.float32
BF16 = jnp.bfloat16

D_MODEL = 1024
N_HEADS = 8
D_HEAD = 64
D_SB = N_HEADS * D_HEAD
POOL_WINDOWS = (2, 4, 8, 16)
D_POOL_GROUP = 128
D_POOL = len(POOL_WINDOWS) * D_POOL_GROUP
POOL_BUF = 15
POOL_HALO = 16
N_KEYS = 128
PEER_HEADS = 8
D_SUBKEY = 128
TOPK = 16
N_SEL = PEER_HEADS * TOPK
N_MOD = 6
EPS = 1e-6
ROW_SUBLANES = 8
LANES = 128
LOG_KEEP_FLOOR = -100.0
VMEM_LIMIT = 56 * 1024 * 1024


def _cparams(sem):
    return pltpu.CompilerParams(dimension_semantics=sem, vmem_limit_bytes=VMEM_LIMIT)


def _rms(x, g):
    return x * lax.rsqrt(jnp.mean(x * x, axis=-1, keepdims=True) + EPS) * g


def _dot(a, b):
    return jnp.dot(a, b, preferred_element_type=F32)


def _dot_nt(a, b):
    return lax.dot_general(a, b, (((1,), (1,)), ((), ())), preferred_element_type=F32)


def _split_bf16(x):
    hi = x.astype(BF16)
    lo = (x - hi.astype(F32)).astype(BF16)
    return hi, lo


def _ada_kernel(c_ref, w_ref, b_ref, o_ref):
    o_ref[...] = _dot(c_ref[...].astype(BF16), w_ref[...].astype(BF16)) + b_ref[...]


def _ada(c, w_ada, b_ada):
    n, d = c.shape
    nout = w_ada.shape[1]
    tn = 1024
    return pl.pallas_call(
        _ada_kernel,
        grid=(nout // tn,),
        in_specs=[pl.BlockSpec((n, d), lambda j: (0, 0)),
                  pl.BlockSpec((d, tn), lambda j: (0, j)),
                  pl.BlockSpec((1, tn), lambda j: (0, j))],
        out_specs=pl.BlockSpec((n, tn), lambda j: (0, j)),
        out_shape=jax.ShapeDtypeStruct((n, nout), F32),
        compiler_params=_cparams(("arbitrary",)),
        name="ada",
    )(c, w_ada, b_ada.reshape(1, nout))


def _qkvp_kernel(x_ref, mod_ref, g_ref, w_ref, q_ref, k_ref, v_ref, p_ref):
    x = x_ref[0]
    shift1 = mod_ref[0, 0:1, :]
    scale1 = mod_ref[0, 1:2, :]
    h = _rms(x, g_ref[...]) * (1.0 + scale1) + shift1
    proj = _dot(h.astype(BF16), w_ref[...])
    for hh in range(N_HEADS):
        q_ref[0, hh] = proj[:, hh * D_HEAD:(hh + 1) * D_HEAD]
        k_ref[0, hh] = proj[:, D_SB + hh * D_HEAD:D_SB + (hh + 1) * D_HEAD]
        v_ref[0, hh] = proj[:, 2 * D_SB + hh * D_HEAD:2 * D_SB + (hh + 1) * D_HEAD]
    p_ref[0] = proj[:, 3 * D_SB:3 * D_SB + D_POOL]


def _qkvp(x, mod, g_mix, w_qkvp, tT):
    B, T, D = x.shape
    nw = w_qkvp.shape[1]
    head_spec = pl.BlockSpec((1, N_HEADS, tT, D_HEAD), lambda b, t: (b, 0, t, 0))
    head_shape = jax.ShapeDtypeStruct((B, N_HEADS, T, D_HEAD), F32)
    return pl.pallas_call(
        _qkvp_kernel,
        grid=(B, T // tT),
        in_specs=[pl.BlockSpec((1, tT, D), lambda b, t: (b, t, 0)),
                  pl.BlockSpec((1, N_MOD, D), lambda b, t: (b, 0, 0)),
                  pl.BlockSpec((1, D), lambda b, t: (0, 0)),
                  pl.BlockSpec((D, nw), lambda b, t: (0, 0))],
        out_specs=[head_spec, head_spec, head_spec,
                   pl.BlockSpec((1, tT, D_POOL), lambda b, t: (b, t, 0))],
        out_shape=[head_shape, head_shape, head_shape,
                   jax.ShapeDtypeStruct((B, T, D_POOL), F32)],
        compiler_params=_cparams(("parallel", "arbitrary")),
        name="qkvp",
    )(x, mod, g_mix, w_qkvp)


def _attn_kernel(q_ref, kd_ref, vd_ref, kp_ref, vp_ref, o_ref, acc_ref, carry_ref, *, tq, tk, n_prev):
    q = (q_ref[0, 0] * (D_HEAD ** -0.5)).astype(BF16)

    def block(kb, vb, tkb, masked, first):
        z = _dot_nt(q, kb.astype(BF16))
        sp = jnp.maximum(z, 0.0) + jnp.log1p(jnp.exp(-jnp.abs(z)))
        log_keep = -sp
        log_beta = z - sp
        if masked:
            row = lax.broadcasted_iota(jnp.int32, (tq, tkb), 0)
            col = lax.broadcasted_iota(jnp.int32, (tq, tkb), 1)
            mask = col < row
            log_keep = jnp.where(mask, log_keep, 0.0)
        r2 = lax.broadcasted_iota(jnp.int32, (tkb, tkb), 0)
        c2 = lax.broadcasted_iota(jnp.int32, (tkb, tkb), 1)
        tri = jnp.where(r2 > c2, 1.0, 0.0).astype(BF16)
        hi, lo = _split_bf16(log_keep)
        suffix = _dot(hi, tri) + _dot(lo, tri)
        total = suffix[:, 0:1] + log_keep[:, 0:1]
        if first:
            a = jnp.exp(log_beta + suffix)
        else:
            a = jnp.exp(log_beta + suffix + carry_ref[...])
        if masked:
            a = jnp.where(mask, a, 0.0)
        contrib = _dot(a.astype(BF16), vb.astype(BF16))
        if first:
            acc_ref[...] = contrib
            carry_ref[...] = total
        else:
            acc_ref[...] += contrib
            carry_ref[...] += total

    block(kd_ref[0, 0], vd_ref[0, 0], tq, True, True)

    n_blocks = pl.program_id(2) if n_prev is None else jnp.int32(n_prev)

    def cond(c):
        j, mx = c
        return jnp.logical_and(j >= 0, mx > LOG_KEEP_FLOOR)

    def body(c):
        j, _ = c
        start = pl.multiple_of(j * tk, tk)
        block(kp_ref[0, 0, pl.ds(start, tk), :], vp_ref[0, 0, pl.ds(start, tk), :], tk, False, False)
        return j - 1, jnp.max(carry_ref[...])

    lax.while_loop(cond, body, (n_blocks - 1, jnp.max(carry_ref[...])))
    o_ref[0, 0] = acc_ref[...]


def _attention(q, k, v, past_k, past_v, tq, tk):
    B, H, T, dh = q.shape
    if past_k is None:
        assert tq == tk
        kp, vp, n_prev = k, v, None
    else:
        assert T == tq
        kp, vp, n_prev = past_k, past_v, past_k.shape[2] // tk
    Tp = kp.shape[2]
    tile = pl.BlockSpec((1, 1, tq, dh), lambda b, h, i: (b, h, i, 0))
    full = pl.BlockSpec((1, 1, Tp, dh), lambda b, h, i: (b, h, 0, 0))
    return pl.pallas_call(
        functools.partial(_attn_kernel, tq=tq, tk=tk, n_prev=n_prev),
        grid=(B, H, T // tq),
        in_specs=[tile, tile, tile, full, full],
        out_specs=tile,
        out_shape=jax.ShapeDtypeStruct((B, H, T, dh), F32),
        scratch_shapes=[pltpu.VMEM((tq, dh), F32), pltpu.VMEM((tq, 1), F32)],
        compiler_params=_cparams(("parallel", "parallel", "arbitrary")),
        name="attn",
    )(q, k, v, kp, vp)


def _mix_kernel(x_ref, mod_ref, gmix_ref, gffn_ref, o_ref, p_ref, halo_ref, pre_ref,
                wg_ref, wgrp_ref, pscale_ref, wsb_ref, wpu_ref, wout_ref, wq_ref, sk_ref,
                x1_ref, h2_ref, st_ref, *, tT, n_valid):
    t = pl.program_id(1)
    x = x_ref[0]
    shift1 = mod_ref[0, 0:1, :]
    scale1 = mod_ref[0, 1:2, :]
    gate1 = mod_ref[0, 2:3, :]
    shift2 = mod_ref[0, 3:4, :]
    scale2 = mod_ref[0, 4:5, :]
    h = _rms(x, gmix_ref[...]) * (1.0 + scale1) + shift1
    g = _dot(h.astype(BF16), wg_ref[...])

    y_sb = _dot(o_ref[0, 0].astype(BF16), wsb_ref[0])
    for hh in range(1, N_HEADS):
        y_sb += _dot(o_ref[0, hh].astype(BF16), wsb_ref[hh])

    p = p_ref[0]
    halo = jnp.where(t == 0, pre_ref[0], halo_ref[0])
    full = jnp.concatenate([halo, p], axis=0)
    pos = t * tT + lax.broadcasted_iota(jnp.int32, (tT, 1), 0)
    pooled = []
    for gi, w in enumerate(POOL_WINDOWS):
        sl = slice(gi * D_POOL_GROUP, (gi + 1) * D_POOL_GROUP)
        a = full[:, sl]
        sh = 1
        while sh < w:
            a = a + pltpu.roll(a, sh, axis=0)
            sh *= 2
        cnt = jnp.minimum(n_valid + pos + 1, w).astype(F32)
        pm = a[POOL_HALO:, :] / cnt - p[:, sl]
        pooled.append(_dot(pm.astype(BF16), wgrp_ref[gi]))
    pooled = jnp.concatenate(pooled, axis=1) * pscale_ref[...]
    y_pool = _dot(pooled.astype(BF16), wpu_ref[...])

    merged = jax.nn.sigmoid(g[:, :D_MODEL]) * y_sb + jax.nn.sigmoid(g[:, D_MODEL:]) * y_pool
    x1 = x + gate1 * _dot(merged.astype(BF16), wout_ref[...])
    x1_ref[0] = x1
    h2 = _rms(x1, gffn_ref[...]) * (1.0 + scale2) + shift2
    h2_ref[0] = h2
    qry = _dot(h2.astype(BF16), wq_ref[...])
    for hp in range(2 * PEER_HEADS):
        qs = qry[:, hp * D_SUBKEY:(hp + 1) * D_SUBKEY].astype(BF16)
        st_ref[0, hp] = _dot_nt(sk_ref[hp % 2], qs)


def _mix(x, mod, g_mix, g_ffn, o, p, prefix, n_valid, w_g, w_grp, pool_scale, w_sb, w_pu, w_out, w_q, sk, tT):
    B, T, D = x.shape
    hb = tT // POOL_HALO
    const = lambda shape: pl.BlockSpec(shape, lambda b, t: (0,) * len(shape))
    tok = lambda last: pl.BlockSpec((1, tT, last), lambda b, t: (b, t, 0))
    return pl.pallas_call(
        functools.partial(_mix_kernel, tT=tT, n_valid=n_valid),
        grid=(B, T // tT),
        in_specs=[tok(D),
                  pl.BlockSpec((1, N_MOD, D), lambda b, t: (b, 0, 0)),
                  const((1, D)), const((1, D)),
                  pl.BlockSpec((1, N_HEADS, tT, D_HEAD), lambda b, t: (b, 0, t, 0)),
                  tok(D_POOL),
                  pl.BlockSpec((1, POOL_HALO, D_POOL), lambda b, t: (b, jnp.maximum(t * hb - 1, 0), 0)),
                  pl.BlockSpec((1, POOL_HALO, D_POOL), lambda b, t: (b, 0, 0)),
                  const(w_g.shape), const(w_grp.shape), const((1, D_POOL)), const(w_sb.shape),
                  const(w_pu.shape), const(w_out.shape), const(w_q.shape), const(sk.shape)],
        out_specs=[tok(D), tok(D),
                   pl.BlockSpec((1, 2 * PEER_HEADS, N_KEYS, tT), lambda b, t: (b, 0, 0, t))],
        out_shape=[jax.ShapeDtypeStruct((B, T, D), F32), jax.ShapeDtypeStruct((B, T, D), F32),
                   jax.ShapeDtypeStruct((B, 2 * PEER_HEADS, N_KEYS, T), F32)],
        compiler_params=_cparams(("parallel", "arbitrary")),
        name="mix",
    )(x, mod, g_mix, g_ffn, o, p, p, prefix, w_g, w_grp, pool_scale, w_sb, w_pu, w_out, w_q, sk)


def _take_top(s, n, payload=None):
    rows = s.shape[0]
    row = lax.broadcasted_iota(jnp.int32, s.shape, 0)
    vals, pays = [], []
    for _ in range(n):
        m = jnp.max(s, axis=0, keepdims=True)
        pick = jnp.min(jnp.where(s == m, row, rows), axis=0, keepdims=True)
        hit = row == pick
        vals.append(m)
        pays.append(pick if payload is None else jnp.max(jnp.where(hit, payload, -1), axis=0, keepdims=True))
        s = jnp.where(hit, -jnp.inf, s)
    return jnp.concatenate(vals, axis=0), jnp.concatenate(pays, axis=0)


def _candidate_pieces():
    pieces, a = [], 0
    while a < TOPK:
        nb = TOPK // (a + 1)
        if nb == 1:
            pieces.append((slice(a, TOPK), slice(0, 1)))
            break
        pieces.append((slice(a, a + 1), slice(0, min(TOPK, -(-nb // 8) * 8))))
        a += 1
    return pieces


def _topk_kernel(st_ref, idx_ref, gate_ref):
    pieces = _candidate_pieces()
    for hh in range(PEER_HEADS):
        s1, i1 = _take_top(st_ref[0, 2 * hh], TOPK)
        s2, i2 = _take_top(st_ref[0, 2 * hh + 1], TOPK)
        cand_s = jnp.concatenate([s1[sa] + s2[sb] for sa, sb in pieces], axis=0)
        cand_i = jnp.concatenate([i1[sa] * N_KEYS + i2[sb] for sa, sb in pieces], axis=0)
        top_s, top_i = _take_top(cand_s, TOPK, cand_i)
        e = jnp.exp(top_s - top_s[0:1])
        idx_ref[0, hh] = top_i
        gate_ref[0, hh] = e / jnp.sum(e, axis=0, keepdims=True)


def _topk(st, tN):
    B, _, _, T = st.shape
    out_spec = pl.BlockSpec((1, PEER_HEADS, TOPK, tN), lambda b, t: (b, 0, 0, t))
    return pl.pallas_call(
        _topk_kernel,
        grid=(B, T // tN),
        in_specs=[pl.BlockSpec((1, 2 * PEER_HEADS, N_KEYS, tN), lambda b, t: (b, 0, 0, t))],
        out_specs=[out_spec, out_spec],
        out_shape=[jax.ShapeDtypeStruct((B, PEER_HEADS, TOPK, T), jnp.int32),
                   jax.ShapeDtypeStruct((B, PEER_HEADS, TOPK, T), F32)],
        compiler_params=_cparams(("parallel", "arbitrary")),
        name="topk",
    )(st)


PACKED_SUBLANES = 8
BF16_ROWS = 2 * PACKED_SUBLANES
DMA_QUEUES = 2
PEER_GROUP = 8


def _pack_bf16_rows(t):
    e = t.shape[0]
    pairs = t.astype(BF16).reshape(e, ROW_SUBLANES // 2, 2, LANES).transpose(0, 1, 3, 2)
    return lax.bitcast_convert_type(pairs, jnp.int32)


def _peer_kernel(idx_ref, gate_ref, h_ref, tab_hbm, o_ref, buf, sems, *, tT):
    G = PEER_GROUP
    rows = N_SEL * BF16_ROWS
    r_id = lax.broadcasted_iota(jnp.int32, (ROW_SUBLANES, rows), 1) % BF16_ROWS
    s_id = lax.broadcasted_iota(jnp.int32, (ROW_SUBLANES, rows), 0)
    sub_u = r_id == s_id
    sub_v = r_id == s_id + ROW_SUBLANES
    group = jnp.where(lax.broadcasted_iota(jnp.int32, (N_SEL, rows), 1) // BF16_ROWS
                      == lax.broadcasted_iota(jnp.int32, (N_SEL, rows), 0), 1.0, 0.0).astype(BF16)
    group_t = jnp.where(lax.broadcasted_iota(jnp.int32, (rows, N_SEL), 0) // BF16_ROWS
                        == lax.broadcasted_iota(jnp.int32, (rows, N_SEL), 1), 1.0, 0.0).astype(BF16)

    def row_copy(tok, k, b):
        return pltpu.make_async_copy(tab_hbm.at[idx_ref[tok, k]], buf.at[b, k], sems.at[b])

    def start(g, slot):
        def one_token(t, c):
            for k in range(N_SEL):
                row_copy(g * G + t, k, slot * G + t).start(priority=k % DMA_QUEUES)
            return c

        lax.fori_loop(0, G, one_token, 0)

    def wait(slot):
        for t in range(G):
            b = slot * G + t
            pltpu.make_async_copy(buf.at[b], buf.at[b], sems.at[b]).wait()

    def split_dot(x, w):
        hi, lo = _split_bf16(x)
        r = _dot(jnp.concatenate([hi, lo], axis=0), w)
        return r[:ROW_SUBLANES] + r[ROW_SUBLANES:]

    def gathered(slot, t):
        words = buf[slot * G + t].reshape(N_SEL * PACKED_SUBLANES, LANES)
        return pltpu.bitcast(words, BF16)

    def compute(g, slot):
        base = pl.multiple_of(g * G, G)
        ysum = []
        for t in range(G):
            xt = h_ref[base + t].astype(BF16)
            y = jnp.where(sub_u, _dot_nt(xt, gathered(slot, t)), 0.0)
            ysum.append(jnp.sum(y, axis=0, keepdims=True))
        act = split_dot(jnp.concatenate(ysum, axis=0), group_t)
        gelu = 0.5 * act * (1.0 + lax.erf(act * (2.0 ** -0.5)))
        wgt = gate_ref[pl.ds(base, G), :] * gelu
        wrow = split_dot(wgt, group)
        for t in range(G):
            wmat = jnp.where(sub_v, jnp.broadcast_to(wrow[t:t + 1], (ROW_SUBLANES, rows)), 0.0)
            o_ref[base + t] = split_dot(wmat, gathered(slot, t))

    start(0, 0)

    def pair(i, c):
        g0 = 2 * i
        start(g0 + 1, 1)
        wait(0)
        compute(g0, 0)

        @pl.when(g0 + 2 < tT // G)
        def _():
            start(g0 + 2, 0)

        wait(1)
        compute(g0 + 1, 1)
        return c

    lax.fori_loop(0, tT // (2 * G), pair, 0)


def _peer(idx, gate, h2, table, tT):
    N = idx.shape[0]
    assert PEER_GROUP == ROW_SUBLANES and tT % (2 * PEER_GROUP) == 0 and N % tT == 0
    return pl.pallas_call(
        functools.partial(_peer_kernel, tT=tT),
        grid=(N // tT,),
        in_specs=[pl.BlockSpec((tT, N_SEL), lambda i: (i, 0), memory_space=pltpu.SMEM),
                  pl.BlockSpec((tT, N_SEL), lambda i: (i, 0)),
                  pl.BlockSpec((tT, ROW_SUBLANES, LANES), lambda i: (i, 0, 0)),
                  pl.BlockSpec(memory_space=pl.ANY)],
        out_specs=pl.BlockSpec((tT, ROW_SUBLANES, LANES), lambda i: (i, 0, 0)),
        out_shape=jax.ShapeDtypeStruct((N, ROW_SUBLANES, LANES), F32),
        scratch_shapes=[pltpu.VMEM((2 * PEER_GROUP, N_SEL, PACKED_SUBLANES, LANES), jnp.int32),
                        pltpu.SemaphoreType.DMA((2 * PEER_GROUP,))],
        compiler_params=_cparams(("arbitrary",)),
        name="peer",
    )(idx, gate, h2, table)


def _final_kernel(x1_ref, peer_ref, mod_ref, g_ref, y_ref):
    gate2 = mod_ref[0, 5:6, :]
    y_ref[0] = _rms(x1_ref[0] + gate2 * peer_ref[0], g_ref[...])


def _final(x1, peer_out, mod, g_final, tT):
    B, T, D = x1.shape
    tok = pl.BlockSpec((1, tT, D), lambda b, t: (b, t, 0))
    return pl.pallas_call(
        _final_kernel,
        grid=(B, T // tT),
        in_specs=[tok, tok, pl.BlockSpec((1, N_MOD, D), lambda b, t: (b, 0, 0)),
                  pl.BlockSpec((1, D), lambda b, t: (0, 0))],
        out_specs=tok,
        out_shape=jax.ShapeDtypeStruct((B, T, D), F32),
        compiler_params=_cparams(("parallel", "arbitrary")),
        name="final",
    )(x1, peer_out, mod, g_final)


def _group(x, mod, past_k, past_v, prefix, n_valid, wts, tiles):
    B, T, D = x.shape
    q, k, v, p = _qkvp(x, mod, wts["g_mix"], wts["w_qkvp"], tiles["proj"])
    o = _attention(q, k, v, past_k, past_v, tiles["tq"], tiles["tk"])
    x1, h2, st = _mix(x, mod, wts["g_mix"], wts["g_ffn"], o, p, prefix, n_valid,
                      wts["w_g"], wts["w_grp"], wts["pool_scale"], wts["w_sb"], wts["w_pu"],
                      wts["w_out"], wts["w_q"], wts["sk"], tiles["mix"])
    idx, gate = _topk(st, tiles["topk"])
    to_rows = lambda a: a.transpose(0, 3, 1, 2).reshape(B * T, N_SEL)
    peer_out = _peer(to_rows(idx), to_rows(gate), h2.reshape(B * T, ROW_SUBLANES, LANES),
                     wts["table"], tiles["peer"])
    y = _final(x1, peer_out.reshape(B, T, D), mod, wts["g_final"], tiles["mix"])
    new_pool = jnp.concatenate([prefix[:, 1:], p], axis=1)[:, -POOL_BUF:]
    return y, k, v, new_pool


def kernel(x_prompt, x_sample, cache_sb_k, cache_sb_v, state_pool, c_prompt, c_sample, w_ada, b_ada, norm_mix_g, norm_ffn_g, w_in, w_pool_grp, pool_scale, w_sb_up, w_pool_up, w_out, peer_w_query, peer_sub_keys, peer_u, peer_v, final_norm_g):
    depth = w_ada.shape[0]
    assert depth == 1
    B, T, D = x_prompt.shape
    Bs, Ts, _ = x_sample.shape
    l = 0
    n_qkvp = 3 * D_SB + D_POOL
    wts = dict(
        g_mix=norm_mix_g[l].reshape(1, D), g_ffn=norm_ffn_g[l].reshape(1, D), g_final=final_norm_g.reshape(1, D),
        w_qkvp=w_in[l][:, :n_qkvp].astype(BF16), w_g=w_in[l][:, n_qkvp:].astype(BF16),
        w_grp=w_pool_grp[l].astype(BF16), pool_scale=pool_scale[l].reshape(1, D_POOL),
        w_sb=w_sb_up[l].astype(BF16).reshape(N_HEADS, D_HEAD, D), w_pu=w_pool_up[l].astype(BF16),
        w_out=w_out[l].astype(BF16), w_q=peer_w_query[l].astype(BF16), sk=peer_sub_keys[l].astype(BF16),
        table=jnp.concatenate([_pack_bf16_rows(peer_u[l]), _pack_bf16_rows(peer_v[l])], axis=1),
    )
    mod = _ada(jnp.concatenate([c_prompt, c_sample], axis=0), w_ada[l], b_ada[l]).reshape(B + Bs, N_MOD, D)

    zero_prefix = jnp.zeros((B, POOL_HALO, D_POOL), F32)
    tiles_p = dict(proj=min(T, 512), tq=min(T, 256), tk=min(T, 256), mix=min(T, 256), topk=min(T, 256), peer=64)
    yp, kp, vp, pp = _group(x_prompt, mod[:B], None, None, zero_prefix, 0, wts, tiles_p)

    prefix_s = jnp.concatenate([jnp.zeros((Bs, POOL_HALO - POOL_BUF, D_POOL), F32), state_pool[l]], axis=1)
    tiles_s = dict(proj=Ts, tq=Ts, tk=min(cache_sb_k.shape[3], 256), mix=Ts, topk=Ts, peer=64)
    ys, ks, vs, ps = _group(x_sample, mod[B:], cache_sb_k[l], cache_sb_v[l], prefix_s, POOL_BUF, wts, tiles_s)

    return (yp, ys, kp[None], vp[None], pp[None], ks[None], vs[None], ps[None])
```

```python
import functools
import math

import jax
import jax.numpy as jnp
from jax import lax
from jax.experimental import pallas as pl
from jax.experimental.pallas import tpu as pltpu

F32 = jnp.float32
BF16 = jnp.bfloat16

D_MODEL = 1024
N_HEADS = 8
D_HEAD = 64
D_SB = N_HEADS * D_HEAD
POOL_WINDOWS = (2, 4, 8, 16)
D_POOL_GROUP = 128
D_POOL = len(POOL_WINDOWS) * D_POOL_GROUP
POOL_BUF = 15
POOL_HALO = 16
N_KEYS = 128
PEER_HEADS = 8
D_SUBKEY = 128
TOPK = 16
N_SEL = PEER_HEADS * TOPK
N_MOD = 6
EPS = 1e-6
ROW_SUBLANES = 8
LANES = 128
LOG_KEEP_FLOOR = -100.0
VMEM_LIMIT = 56 * 1024 * 1024


def _cparams(sem):
    return pltpu.CompilerParams(dimension_semantics=sem, vmem_limit_bytes=VMEM_LIMIT)


def _rms(x, g):
    return x * lax.rsqrt(jnp.mean(x * x, axis=-1, keepdims=True) + EPS) * g


def _dot(a, b):
    return jnp.dot(a, b, preferred_element_type=F32)


def _dot_nt(a, b):
    return lax.dot_general(a, b, (((1,), (1,)), ((), ())), preferred_element_type=F32)


def _split_bf16(x):
    hi = x.astype(BF16)
    lo = (x - hi.astype(F32)).astype(BF16)
    return hi, lo


def _ada_kernel(c_ref, w_ref, b_ref, o_ref):
    o_ref[...] = _dot(c_ref[...].astype(BF16), w_ref[...].astype(BF16)) + b_ref[...]


def _ada(c, w_ada, b_ada):
    n, d = c.shape
    nout = w_ada.shape[1]
    tn = 1024
    return pl.pallas_call(
        _ada_kernel,
        grid=(nout // tn,),
        in_specs=[pl.BlockSpec((n, d), lambda j: (0, 0)),
                  pl.BlockSpec((d, tn), lambda j: (0, j)),
                  pl.BlockSpec((1, tn), lambda j: (0, j))],
        out_specs=pl.BlockSpec((n, tn), lambda j: (0, j)),
        out_shape=jax.ShapeDtypeStruct((n, nout), F32),
        compiler_params=_cparams(("arbitrary",)),
        name="ada",
    )(c, w_ada, b_ada.reshape(1, nout))


def _qkvp_kernel(x_ref, mod_ref, g_ref, w_ref, q_ref, k_ref, v_ref, p_ref):
    x = x_ref[0]
    shift1 = mod_ref[0, 0:1, :]
    scale1 = mod_ref[0, 1:2, :]
    h = _rms(x, g_ref[...]) * (1.0 + scale1) + shift1
    proj = _dot(h.astype(BF16), w_ref[...])
    for hh in range(N_HEADS):
        q_ref[0, hh] = proj[:, hh * D_HEAD:(hh + 1) * D_HEAD]
        k_ref[0, hh] = proj[:, D_SB + hh * D_HEAD:D_SB + (hh + 1) * D_HEAD]
        v_ref[0, hh] = proj[:, 2 * D_SB + hh * D_HEAD:2 * D_SB + (hh + 1) * D_HEAD]
    p_ref[0] = proj[:, 3 * D_SB:3 * D_SB + D_POOL]


def _qkvp(x, mod, g_mix, w_qkvp, tT):
    B, T, D = x.shape
    nw = w_qkvp.shape[1]
    head_spec = pl.BlockSpec((1, N_HEADS, tT, D_HEAD), lambda b, t: (b, 0, t, 0))
    head_shape = jax.ShapeDtypeStruct((B, N_HEADS, T, D_HEAD), F32)
    return pl.pallas_call(
        _qkvp_kernel,
        grid=(B, T // tT),
        in_specs=[pl.BlockSpec((1, tT, D), lambda b, t: (b, t, 0)),
                  pl.BlockSpec((1, N_MOD, D), lambda b, t: (b, 0, 0)),
                  pl.BlockSpec((1, D), lambda b, t: (0, 0)),
                  pl.BlockSpec((D, nw), lambda b, t: (0, 0))],
        out_specs=[head_spec, head_spec, head_spec,
                   pl.BlockSpec((1, tT, D_POOL), lambda b, t: (b, t, 0))],
        out_shape=[head_shape, head_shape, head_shape,
                   jax.ShapeDtypeStruct((B, T, D_POOL), F32)],
        compiler_params=_cparams(("parallel", "arbitrary")),
        name="qkvp",
    )(x, mod, g_mix, w_qkvp)


def _attn_kernel(q_ref, kd_ref, vd_ref, kp_ref, vp_ref, o_ref, acc_ref, carry_ref, *, tq, tk, n_prev):
    q = (q_ref[0, 0] * (D_HEAD ** -0.5)).astype(BF16)

    def block(kb, vb, tkb, masked, first):
        z = _dot_nt(q, kb.astype(BF16))
        sp = jnp.maximum(z, 0.0) + jnp.log1p(jnp.exp(-jnp.abs(z)))
        log_keep = -sp
        log_beta = z - sp
        if masked:
            row = lax.broadcasted_iota(jnp.int32, (tq, tkb), 0)
            col = lax.broadcasted_iota(jnp.int32, (tq, tkb), 1)
            mask = col < row
            log_keep = jnp.where(mask, log_keep, 0.0)
        r2 = lax.broadcasted_iota(jnp.int32, (tkb, tkb), 0)
        c2 = lax.broadcasted_iota(jnp.int32, (tkb, tkb), 1)
        tri = jnp.where(r2 > c2, 1.0, 0.0).astype(BF16)
        hi, lo = _split_bf16(log_keep)
        suffix = _dot(hi, tri) + _dot(lo, tri)
        total = suffix[:, 0:1] + log_keep[:, 0:1]
        if first:
            a = jnp.exp(log_beta + suffix)
        else:
            a = jnp.exp(log_beta + suffix + carry_ref[...])
        if masked:
            a = jnp.where(mask, a, 0.0)
        contrib = _dot(a.astype(BF16), vb.astype(BF16))
        if first:
            acc_ref[...] = contrib
            carry_ref[...] = total
        else:
            acc_ref[...] += contrib
            carry_ref[...] += total

    block(kd_ref[0, 0], vd_ref[0, 0], tq, True, True)

    n_blocks = pl.program_id(2) if n_prev is None else jnp.int32(n_prev)

    def cond(c):
        j, mx = c
        return jnp.logical_and(j >= 0, mx > LOG_KEEP_FLOOR)

    def body(c):
        j, _ = c
        start = pl.multiple_of(j * tk, tk)
        block(kp_ref[0, 0, pl.ds(start, tk), :], vp_ref[0, 0, pl.ds(start, tk), :], tk, False, False)
        return j - 1, jnp.max(carry_ref[...])

    lax.while_loop(cond, body, (n_blocks - 1, jnp.max(carry_ref[...])))
    o_ref[0, 0] = acc_ref[...]


def _attention(q, k, v, past_k, past_v, tq, tk):
    B, H, T, dh = q.shape
    if past_k is None:
        assert tq == tk
        kp, vp, n_prev = k, v, None
    else:
        assert T == tq
        kp, vp, n_prev = past_k, past_v, past_k.shape[2] // tk
    Tp = kp.shape[2]
    tile = pl.BlockSpec((1, 1, tq, dh), lambda b, h, i: (b, h, i, 0))
    full = pl.BlockSpec((1, 1, Tp, dh), lambda b, h, i: (b, h, 0, 0))
    return pl.pallas_call(
        functools.partial(_attn_kernel, tq=tq, tk=tk, n_prev=n_prev),
        grid=(B, H, T // tq),
        in_specs=[tile, tile, tile, full, full],
        out_specs=tile,
        out_shape=jax.ShapeDtypeStruct((B, H, T, dh), F32),
        scratch_shapes=[pltpu.VMEM((tq, dh), F32), pltpu.VMEM((tq, 1), F32)],
        compiler_params=_cparams(("parallel", "parallel", "arbitrary")),
        name="attn",
    )(q, k, v, kp, vp)


def _mix_kernel(x_ref, mod_ref, gmix_ref, gffn_ref, o_ref, p_ref, halo_ref, pre_ref,
                wg_ref, wgrp_ref, pscale_ref, wsb_ref, wpu_ref, wout_ref, wq_ref, sk_ref,
                x1_ref, h2_ref, st_ref, *, tT, n_valid):
    t = pl.program_id(1)
    x = x_ref[0]
    shift1 = mod_ref[0, 0:1, :]
    scale1 = mod_ref[0, 1:2, :]
    gate1 = mod_ref[0, 2:3, :]
    shift2 = mod_ref[0, 3:4, :]
    scale2 = mod_ref[0, 4:5, :]
    h = _rms(x, gmix_ref[...]) * (1.0 + scale1) + shift1
    g = _dot(h.astype(BF16), wg_ref[...])

    y_sb = _dot(o_ref[0, 0].astype(BF16), wsb_ref[0])
    for hh in range(1, N_HEADS):
        y_sb += _dot(o_ref[0, hh].astype(BF16), wsb_ref[hh])

    p = p_ref[0]
    halo = jnp.where(t == 0, pre_ref[0], halo_ref[0])
    full = jnp.concatenate([halo, p], axis=0)
    pos = t * tT + lax.broadcasted_iota(jnp.int32, (tT, 1), 0)
    pooled = []
    for gi, w in enumerate(POOL_WINDOWS):
        sl = slice(gi * D_POOL_GROUP, (gi + 1) * D_POOL_GROUP)
        a = full[:, sl]
        sh = 1
        while sh < w:
            a = a + pltpu.roll(a, sh, axis=0)
            sh *= 2
        cnt = jnp.minimum(n_valid + pos + 1, w).astype(F32)
        pm = a[POOL_HALO:, :] / cnt - p[:, sl]
        pooled.append(_dot(pm.astype(BF16), wgrp_ref[gi]))
    pooled = jnp.concatenate(pooled, axis=1) * pscale_ref[...]
    y_pool = _dot(pooled.astype(BF16), wpu_ref[...])

    merged = jax.nn.sigmoid(g[:, :D_MODEL]) * y_sb + jax.nn.sigmoid(g[:, D_MODEL:]) * y_pool
    x1 = x + gate1 * _dot(merged.astype(BF16), wout_ref[...])
    x1_ref[0] = x1
    h2 = _rms(x1, gffn_ref[...]) * (1.0 + scale2) + shift2
    h2_ref[0] = h2
    qry = _dot(h2.astype(BF16), wq_ref[...])
    for hp in range(2 * PEER_HEADS):
        qs = qry[:, hp * D_SUBKEY:(hp + 1) * D_SUBKEY].astype(BF16)
        st_ref[0, hp] = _dot_nt(sk_ref[hp % 2], qs)


def _mix(x, mod, g_mix, g_ffn, o, p, prefix, n_valid, w_g, w_grp, pool_scale, w_sb, w_pu, w_out, w_q, sk, tT):
    B, T, D = x.shape
    hb = tT // POOL_HALO
    const = lambda shape: pl.BlockSpec(shape, lambda b, t: (0,) * len(shape))
    tok = lambda last: pl.BlockSpec((1, tT, last), lambda b, t: (b, t, 0))
    return pl.pallas_call(
        functools.partial(_mix_kernel, tT=tT, n_valid=n_valid),
        grid=(B, T // tT),
        in_specs=[tok(D),
                  pl.BlockSpec((1, N_MOD, D), lambda b, t: (b, 0, 0)),
                  const((1, D)), const((1, D)),
                  pl.BlockSpec((1, N_HEADS, tT, D_HEAD), lambda b, t: (b, 0, t, 0)),
                  tok(D_POOL),
                  pl.BlockSpec((1, POOL_HALO, D_POOL), lambda b, t: (b, jnp.maximum(t * hb - 1, 0), 0)),
                  pl.BlockSpec((1, POOL_HALO, D_POOL), lambda b, t: (b, 0, 0)),
                  const(w_g.shape), const(w_grp.shape), const((1, D_POOL)), const(w_sb.shape),
                  const(w_pu.shape), const(w_out.shape), const(w_q.shape), const(sk.shape)],
        out_specs=[tok(D), tok(D),
                   pl.BlockSpec((1, 2 * PEER_HEADS, N_KEYS, tT), lambda b, t: (b, 0, 0, t))],
        out_shape=[jax.ShapeDtypeStruct((B, T, D), F32), jax.ShapeDtypeStruct((B, T, D), F32),
                   jax.ShapeDtypeStruct((B, 2 * PEER_HEADS, N_KEYS, T), F32)],
        compiler_params=_cparams(("parallel", "arbitrary")),
        name="mix",
    )(x, mod, g_mix, g_ffn, o, p, p, prefix, w_g, w_grp, pool_scale, w_sb, w_pu, w_out, w_q, sk)


def _take_top(s, n, payload=None):
    rows = s.shape[0]
    row = lax.broadcasted_iota(jnp.int32, s.shape, 0)
    vals, pays = [], []
    for _ in range(n):
        m = jnp.max(s, axis=0, keepdims=True)
        pick = jnp.min(jnp.where(s == m, row, rows), axis=0, keepdims=True)
        hit = row == pick
        vals.append(m)
        pays.append(pick if payload is None else jnp.max(jnp.where(hit, payload, -1), axis=0, keepdims=True))
        s = jnp.where(hit, -jnp.inf, s)
    return jnp.concatenate(vals, axis=0), jnp.concatenate(pays, axis=0)


def _candidate_pieces():
    pieces, a = [], 0
    while a < TOPK:
        nb = TOPK // (a + 1)
        if nb == 1:
            pieces.append((slice(a, TOPK), slice(0, 1)))
            break
        pieces.append((slice(a, a + 1), slice(0, min(TOPK, -(-nb // 8) * 8))))
        a += 1
    return pieces


def _topk_kernel(st_ref, idx_ref, gate_ref):
    pieces = _candidate_pieces()
    for hh in range(PEER_HEADS):
        s1, i1 = _take_top(st_ref[0, 2 * hh], TOPK)
        s2, i2 = _take_top(st_ref[0, 2 * hh + 1], TOPK)
        cand_s = jnp.concatenate([s1[sa] + s2[sb] for sa, sb in pieces], axis=0)
        cand_i = jnp.concatenate([i1[sa] * N_KEYS + i2[sb] for sa, sb in pieces], axis=0)
        top_s, top_i = _take_top(cand_s, TOPK, cand_i)
        e = jnp.exp(top_s - top_s[0:1])
        idx_ref[0, hh] = top_i
        gate_ref[0, hh] = e / jnp.sum(e, axis=0, keepdims=True)


def _topk(st, tN):
    B, _, _, T = st.shape
    out_spec = pl.BlockSpec((1, PEER_HEADS, TOPK, tN), lambda b, t: (b, 0, 0, t))
    return pl.pallas_call(
        _topk_kernel,
        grid=(B, T // tN),
        in_specs=[pl.BlockSpec((1, 2 * PEER_HEADS, N_KEYS, tN), lambda b, t: (b, 0, 0, t))],
        out_specs=[out_spec, out_spec],
        out_shape=[jax.ShapeDtypeStruct((B, PEER_HEADS, TOPK, T), jnp.int32),
                   jax.ShapeDtypeStruct((B, PEER_HEADS, TOPK, T), F32)],
        compiler_params=_cparams(("parallel", "arbitrary")),
        name="topk",
    )(st)


PACKED_SUBLANES = 8
BF16_ROWS = 2 * PACKED_SUBLANES
DMA_QUEUES = 2
PEER_GROUP = 8
PEER_SLOTS = 3


def _pack_bf16_rows(t):
    e = t.shape[0]
    pairs = t.astype(BF16).reshape(e, ROW_SUBLANES // 2, 2, LANES).transpose(0, 1, 3, 2)
    return lax.bitcast_convert_type(pairs, jnp.int32)


def _peer_kernel(idx_ref, gate_ref, h_ref, tab_hbm, o_ref, buf0, buf1, buf2, sems, *, tT):
    bufs = (buf0, buf1, buf2)
    G = PEER_GROUP
    n_groups = tT // G
    rows = N_SEL * BF16_ROWS
    r_id = lax.broadcasted_iota(jnp.int32, (ROW_SUBLANES, rows), 1) % BF16_ROWS
    s_id = lax.broadcasted_iota(jnp.int32, (ROW_SUBLANES, rows), 0)
    sub_u = r_id == s_id
    sub_v = r_id == s_id + ROW_SUBLANES
    group = jnp.where(lax.broadcasted_iota(jnp.int32, (N_SEL, rows), 1) // BF16_ROWS
                      == lax.broadcasted_iota(jnp.int32, (N_SEL, rows), 0), 1.0, 0.0).astype(BF16)
    group_t = jnp.where(lax.broadcasted_iota(jnp.int32, (rows, N_SEL), 0) // BF16_ROWS
                        == lax.broadcasted_iota(jnp.int32, (rows, N_SEL), 1), 1.0, 0.0).astype(BF16)

    def issue_token(tok, slot, t):
        for k in range(N_SEL):
            pltpu.make_async_copy(tab_hbm.at[idx_ref[tok, k]], bufs[slot].at[t, k],
                                  sems.at[slot * G + t]).start(priority=k % DMA_QUEUES)

    def start(g, slot):
        def one_token(t, c):
            issue_token(g * G + t, slot, t)
            return c

        lax.fori_loop(0, G, one_token, 0)

    def wait(slot):
        for t in range(G):
            pltpu.make_async_copy(bufs[slot].at[t], bufs[slot].at[t], sems.at[slot * G + t]).wait()

    def split_dot(x, w):
        hi, lo = _split_bf16(x)
        r = _dot(jnp.concatenate([hi, lo], axis=0), w)
        return r[:ROW_SUBLANES] + r[ROW_SUBLANES:]

    def gathered(slot, t):
        words = bufs[slot][t].reshape(N_SEL * PACKED_SUBLANES, LANES)
        return pltpu.bitcast(words, BF16)

    def compute(g, slot, prefetch_slot=None):
        base = pl.multiple_of(g * G, G)
        ysum = []
        for t in range(G):
            if prefetch_slot is not None:
                issue_token(base + 2 * G + t, prefetch_slot, t)
            xt = h_ref[base + t].astype(BF16)
            y = jnp.where(sub_u, _dot_nt(xt, gathered(slot, t)), 0.0)
            ysum.append(jnp.sum(y, axis=0, keepdims=True))
        act = split_dot(jnp.concatenate(ysum, axis=0), group_t)
        gelu = 0.5 * act * (1.0 + lax.erf(act * (2.0 ** -0.5)))
        wgt = gate_ref[pl.ds(base, G), :] * gelu
        wrow = split_dot(wgt, group)
        for t in range(G):
            wmat = jnp.where(sub_v, jnp.broadcast_to(wrow[t:t + 1], (ROW_SUBLANES, rows)), 0.0)
            o_ref[base + t] = split_dot(wmat, gathered(slot, t))

    start(0, 0)
    start(1, 1)

    def triple(i, c):
        for s in range(PEER_SLOTS):
            wait(s)
            compute(PEER_SLOTS * i + s, s, (s + 2) % PEER_SLOTS)
        return c

    lax.fori_loop(0, (n_groups - 2) // PEER_SLOTS, triple, 0)
    for g in (n_groups - 2, n_groups - 1):
        wait(g % PEER_SLOTS)
        compute(g, g % PEER_SLOTS)


def _peer(idx, gate, h2, table, tT):
    N = idx.shape[0]
    assert PEER_GROUP == ROW_SUBLANES and N % tT == 0 and tT % PEER_GROUP == 0
    assert (tT // PEER_GROUP - 2) % PEER_SLOTS == 0
    return pl.pallas_call(
        functools.partial(_peer_kernel, tT=tT),
        grid=(N // tT,),
        in_specs=[pl.BlockSpec((tT, N_SEL), lambda i: (i, 0), memory_space=pltpu.SMEM),
                  pl.BlockSpec((tT, N_SEL), lambda i: (i, 0)),
                  pl.BlockSpec((tT, ROW_SUBLANES, LANES), lambda i: (i, 0, 0)),
                  pl.BlockSpec(memory_space=pl.ANY)],
        out_specs=pl.BlockSpec((tT, ROW_SUBLANES, LANES), lambda i: (i, 0, 0)),
        out_shape=jax.ShapeDtypeStruct((N, ROW_SUBLANES, LANES), F32),
        scratch_shapes=[pltpu.VMEM((PEER_GROUP, N_SEL, PACKED_SUBLANES, LANES), jnp.int32)] * PEER_SLOTS
        + [pltpu.SemaphoreType.DMA((PEER_SLOTS * PEER_GROUP,))],
        compiler_params=_cparams(("arbitrary",)),
        name="peer",
    )(idx, gate, h2, table)


def _final_kernel(x1_ref, peer_ref, mod_ref, g_ref, y_ref):
    gate2 = mod_ref[0, 5:6, :]
    y_ref[0] = _rms(x1_ref[0] + gate2 * peer_ref[0], g_ref[...])


def _final(x1, peer_out, mod, g_final, tT):
    B, T, D = x1.shape
    tok = pl.BlockSpec((1, tT, D), lambda b, t: (b, t, 0))
    return pl.pallas_call(
        _final_kernel,
        grid=(B, T // tT),
        in_specs=[tok, tok, pl.BlockSpec((1, N_MOD, D), lambda b, t: (b, 0, 0)),
                  pl.BlockSpec((1, D), lambda b, t: (0, 0))],
        out_specs=tok,
        out_shape=jax.ShapeDtypeStruct((B, T, D), F32),
        compiler_params=_cparams(("parallel", "arbitrary")),
        name="final",
    )(x1, peer_out, mod, g_final)


def _group(x, mod, past_k, past_v, prefix, n_valid, wts, tiles):
    B, T, D = x.shape
    q, k, v, p = _qkvp(x, mod, wts["g_mix"], wts["w_qkvp"], tiles["proj"])
    o = _attention(q, k, v, past_k, past_v, tiles["tq"], tiles["tk"])
    x1, h2, st = _mix(x, mod, wts["g_mix"], wts["g_ffn"], o, p, prefix, n_valid,
                      wts["w_g"], wts["w_grp"], wts["pool_scale"], wts["w_sb"], wts["w_pu"],
                      wts["w_out"], wts["w_q"], wts["sk"], tiles["mix"])
    idx, gate = _topk(st, tiles["topk"])
    to_rows = lambda a: a.transpose(0, 3, 1, 2).reshape(B * T, N_SEL)
    peer_out = _peer(to_rows(idx), to_rows(gate), h2.reshape(B * T, ROW_SUBLANES, LANES),
                     wts["table"], tiles["peer"])
    y = _final(x1, peer_out.reshape(B, T, D), mod, wts["g_final"], tiles["mix"])
    new_pool = jnp.concatenate([prefix[:, 1:], p], axis=1)[:, -POOL_BUF:]
    return y, k, v, new_pool


def kernel(x_prompt, x_sample, cache_sb_k, cache_sb_v, state_pool, c_prompt, c_sample, w_ada, b_ada, norm_mix_g, norm_ffn_g, w_in, w_pool_grp, pool_scale, w_sb_up, w_pool_up, w_out, peer_w_query, peer_sub_keys, peer_u, peer_v, final_norm_g):
    depth = w_ada.shape[0]
    assert depth == 1
    B, T, D = x_prompt.shape
    Bs, Ts, _ = x_sample.shape
    l = 0
    n_qkvp = 3 * D_SB + D_POOL
    wts = dict(
        g_mix=norm_mix_g[l].reshape(1, D), g_ffn=norm_ffn_g[l].reshape(1, D), g_final=final_norm_g.reshape(1, D),
        w_qkvp=w_in[l][:, :n_qkvp].astype(BF16), w_g=w_in[l][:, n_qkvp:].astype(BF16),
        w_grp=w_pool_grp[l].astype(BF16), pool_scale=pool_scale[l].reshape(1, D_POOL),
        w_sb=w_sb_up[l].astype(BF16).reshape(N_HEADS, D_HEAD, D), w_pu=w_pool_up[l].astype(BF16),
        w_out=w_out[l].astype(BF16), w_q=peer_w_query[l].astype(BF16), sk=peer_sub_keys[l].astype(BF16),
        table=jnp.concatenate([_pack_bf16_rows(peer_u[l]), _pack_bf16_rows(peer_v[l])], axis=1),
    )
    mod = _ada(jnp.concatenate([c_prompt, c_sample], axis=0), w_ada[l], b_ada[l]).reshape(B + Bs, N_MOD, D)

    zero_prefix = jnp.zeros((B, POOL_HALO, D_POOL), F32)
    tiles_p = dict(proj=min(T, 512), tq=min(T, 256), tk=min(T, 256), mix=min(T, 256), topk=min(T, 256), peer=64)
    yp, kp, vp, pp = _group(x_prompt, mod[:B], None, None, zero_prefix, 0, wts, tiles_p)

    prefix_s = jnp.concatenate([jnp.zeros((Bs, POOL_HALO - POOL_BUF, D_POOL), F32), state_pool[l]], axis=1)
    tiles_s = dict(proj=Ts, tq=Ts, tk=min(cache_sb_k.shape[3], 256), mix=Ts, topk=Ts, peer=64)
    ys, ks, vs, ps = _group(x_sample, mod[B:], cache_sb_k[l], cache_sb_v[l], prefix_s, POOL_BUF, wts, tiles_s)

    return (yp, ys, kp[None], vp[None], pp[None], ks[None], vs[None], ps[None])
```
